```python
import math
import jax, jax.numpy as jnp
from jax import lax
import numpy as np

D_MODEL = 1024
BATCH = 16
SEQ = 4096
DEPTH = 4

CTX_LEN = 256
GRID_W = 64
HEAD_DIM = 64
DIFF_HEADS = 8
DIFF_V_DIM = 2 * HEAD_DIM
SWA_Q_HEADS = 16
SWA_KV_HEADS = 4
SWA_GROUP = SWA_Q_HEADS // SWA_KV_HEADS
WINDOW = 128
Q_BLOCK = 128
N_EXPERTS = 16
EXPERT_FF = 2816
CAPACITY_FACTOR = 2
ROPE_THETA = 10000.0
EPS = 1e-6
NEG_INF = -1e30
ADA_INIT_SCALE = 0.5
F32 = jnp.float32

DIFF_Q_W = DIFF_HEADS * 2 * HEAD_DIM
DIFF_K_W = DIFF_HEADS * 2 * HEAD_DIM
DIFF_V_W = DIFF_HEADS * DIFF_V_DIM
SWA_Q_W = SWA_Q_HEADS * HEAD_DIM
SWA_KV_W = SWA_KV_HEADS * HEAD_DIM
IN_SPLITS = (DIFF_Q_W, DIFF_K_W, DIFF_V_W, SWA_Q_W, SWA_KV_W, SWA_KV_W, D_MODEL, D_MODEL)
IN_W = sum(IN_SPLITS)

kernel_name = "hybrid_diff_swa_ecmoe_dit"


def rmsnorm(x, g):
    xf = x.astype(F32)
    y = xf * lax.rsqrt(jnp.mean(xf * xf, axis=-1, keepdims=True) + EPS)
    return (y * g.astype(F32)).astype(x.dtype)


def modulate(x, g, shift, scale):
    return rmsnorm(x, g) * (1 + scale) + shift


def axial_rope_tables(n):
    rows = n // GRID_W
    row = jnp.repeat(jnp.arange(rows), GRID_W).astype(F32)
    col = jnp.tile(jnp.arange(GRID_W), rows).astype(F32)
    half = HEAD_DIM // 2
    inv_freq = ROPE_THETA ** (-jnp.arange(0, half, 2, dtype=F32) / half)
    ang = jnp.stack([row[:, None] * inv_freq, col[:, None] * inv_freq], axis=1)
    return jnp.cos(ang), jnp.sin(ang)


def apply_rope(x, rope):
    cos, sin = rope
    shp = x.shape
    xr = x.astype(F32).reshape(shp[:-1] + (2, 2, HEAD_DIM // 4))
    bshape = (shp[1],) + (1,) * (x.ndim - 3) + (2, HEAD_DIM // 4)
    cs, sn = cos.reshape(bshape), sin.reshape(bshape)
    x1, x2 = xr[..., 0, :], xr[..., 1, :]
    out = jnp.stack([x1 * cs - x2 * sn, x1 * sn + x2 * cs], axis=-2)
    return out.reshape(shp).astype(x.dtype)


def split_in(p):
    offs = tuple(int(o) for o in np.cumsum(IN_SPLITS)[:-1])
    return jnp.split(p, offs, axis=-1)


def diff_qkv(q, k, v, qg, kg, rope):
    B, n = q.shape[:2]
    q = rmsnorm(q.reshape(B, n, DIFF_HEADS, 2, HEAD_DIM), qg)
    k = rmsnorm(k.reshape(B, n, DIFF_HEADS, 2, HEAD_DIM), kg)
    if rope is not None:
        q, k = apply_rope(q, rope), apply_rope(k, rope)
    return q, k, v.reshape(B, n, DIFF_HEADS, DIFF_V_DIM)


def diff_attend(q, k, v, lam):
    s = jnp.einsum('bqhcd,bkhcd->bhcqk', q, k).astype(F32) * (HEAD_DIM ** -0.5)
    p = jax.nn.softmax(s, axis=-1)
    a = p[:, :, 0] - lam * p[:, :, 1]
    return jnp.einsum('bhqk,bkhd->bqhd', a.astype(v.dtype), v)


def diff_attn_latent(q, k_all, v_all, lam):
    B, N = q.shape[:2]
    nb = N // Q_BLOCK
    qb = jnp.swapaxes(q.reshape((B, nb, Q_BLOCK) + q.shape[2:]), 0, 1)
    o = lax.map(lambda qi: diff_attend(qi, k_all, v_all, lam), qb)
    return jnp.swapaxes(o, 0, 1).reshape(B, N, DIFF_HEADS, DIFF_V_DIM)


def swa_qkv(q, k, v, qg, kg, rope):
    B, n = q.shape[:2]
    q = rmsnorm(q.reshape(B, n, SWA_Q_HEADS, HEAD_DIM), qg)
    k = rmsnorm(k.reshape(B, n, SWA_KV_HEADS, HEAD_DIM), kg)
    if rope is not None:
        q, k = apply_rope(q, rope), apply_rope(k, rope)
    return q, k, v.reshape(B, n, SWA_KV_HEADS, HEAD_DIM)


def sink_attend(q, k, v, sink):
    B, L = q.shape[:2]
    qg = q.reshape(B, L, SWA_KV_HEADS, SWA_GROUP, HEAD_DIM)
    s = jnp.einsum('bqhgd,bkhd->bhgqk', qg, k).astype(F32) * (HEAD_DIM ** -0.5)
    s_sink = jnp.broadcast_to(sink.astype(F32).reshape(1, SWA_KV_HEADS, SWA_GROUP, 1, 1), s.shape[:-1] + (1,))
    p = jax.nn.softmax(jnp.concatenate([s, s_sink], axis=-1), axis=-1)[..., :-1]
    o = jnp.einsum('bhgqk,bkhd->bqhgd', p.astype(v.dtype), v)
    return o.reshape(B, L, SWA_Q_HEADS, HEAD_DIM)


def swa_latent(q, k, v, k_ctx, v_ctx, sink):
    B, N = q.shape[:2]
    nb = N // Q_BLOCK
    qb = q.reshape(B, nb, Q_BLOCK, SWA_KV_HEADS, SWA_GROUP, HEAD_DIM)

    def band(t):
        tp = jnp.pad(t, ((0, 0), (WINDOW, WINDOW), (0, 0), (0, 0)))
        tp = tp.reshape(B, nb + 2, Q_BLOCK, SWA_KV_HEADS, HEAD_DIM)
        return jnp.concatenate([tp[:, :-2], tp[:, 1:-1], tp[:, 2:]], axis=2)

    kw, vw = band(k), band(v)
    blk = jnp.arange(nb)[:, None, None] * Q_BLOCK
    qpos = blk + jnp.arange(Q_BLOCK)[None, :, None]
    kpos = blk - WINDOW + jnp.arange(3 * Q_BLOCK)[None, None, :]
    valid = (jnp.abs(qpos - kpos) <= WINDOW) & (kpos >= 0) & (kpos < N)
    scale = HEAD_DIM ** -0.5
    s_win = jnp.einsum('bnqhgd,bnkhd->bnhgqk', qb, kw).astype(F32) * scale
    s_win = jnp.where(valid[None, :, None, None], s_win, NEG_INF)
    s_ctx = jnp.einsum('bnqhgd,bkhd->bnhgqk', qb, k_ctx).astype(F32) * scale
    s_sink = jnp.broadcast_to(sink.astype(F32).reshape(1, 1, SWA_KV_HEADS, SWA_GROUP, 1, 1),
                              s_ctx.shape[:-1] + (1,))
    p = jax.nn.softmax(jnp.concatenate([s_win, s_ctx, s_sink], axis=-1), axis=-1)
    kwl = 3 * Q_BLOCK
    p_win = p[..., :kwl].astype(v.dtype)
    p_ctx = p[..., kwl:kwl + k_ctx.shape[1]].astype(v.dtype)
    o = (jnp.einsum('bnhgqk,bnkhd->bnqhgd', p_win, vw)
         + jnp.einsum('bnhgqk,bkhd->bnqhgd', p_ctx, v_ctx))
    return o.reshape(B, N, SWA_Q_HEADS, HEAD_DIM)


def expert_choice_ffn(h, w_router, w1, w3, w2):
    B, n, D = h.shape
    cap = CAPACITY_FACTOR * n // N_EXPERTS
    aff = jax.nn.softmax(jnp.einsum('bnd,de->bne', h, w_router).astype(F32), axis=-1)
    gate, idx = lax.top_k(jnp.swapaxes(aff, 1, 2), cap)
    xe = jax.vmap(lambda hb, ib: hb[ib])(h, idx)

    def expert(args):
        xi, a1, a3, a2 = args
        return (jax.nn.silu(xi @ a1) * (xi @ a3)) @ a2

    ye = lax.map(expert, (jnp.swapaxes(xe, 0, 1), w1, w3, w2))
    ye = jnp.swapaxes(ye, 0, 1) * gate[..., None].astype(h.dtype)
    return jax.vmap(lambda ib, yb: jnp.zeros((n, D), h.dtype).at[ib.reshape(-1)].add(yb.reshape(-1, D)))(idx, ye)


def hybrid_layer(xl, xc, sc_l, sc_c, rope, lam_init, last,
                 w_ada, b_ada, norm1_g, w_in, b_gate, diff_q_g, diff_k_g, diff_lambda, diff_subln_g,
                 swa_q_g, swa_k_g, swa_sink, w_branch_a, w_branch_b, w_out, norm2_g,
                 w_router, w_e1, w_e3, w_e2):
    B, N, D = xl.shape
    L = xc.shape[1]
    mod_l = (sc_l @ w_ada + b_ada)[:, None, :]
    mod_c = (sc_c @ w_ada + b_ada)[None, None, :]
    sh1_l, s1_l, g1_l, sh2_l, s2_l, g2_l = jnp.split(mod_l, 6, axis=-1)
    sh1_c, s1_c, g1_c, sh2_c, s2_c, g2_c = jnp.split(mod_c, 6, axis=-1)

    hl = modulate(xl, norm1_g, sh1_l, s1_l)
    hc = modulate(xc, norm1_g, sh1_c, s1_c)
    qa_l, ka_l, va_l, qb_l, kb_l, vb_l, ga_l, gb_l = split_in(hl @ w_in)
    qa_c, ka_c, va_c, qb_c, kb_c, vb_c, ga_c, gb_c = split_in(hc @ w_in)

    lam = (jnp.exp(jnp.sum(diff_lambda[0].astype(F32) * diff_lambda[1].astype(F32)))
           - jnp.exp(jnp.sum(diff_lambda[2].astype(F32) * diff_lambda[3].astype(F32))) + lam_init)

    qa_l, ka_l, va_l = diff_qkv(qa_l, ka_l, va_l, diff_q_g, diff_k_g, rope)
    qa_c, ka_c, va_c = diff_qkv(qa_c, ka_c, va_c, diff_q_g, diff_k_g, None)
    qb_l, kb_l, vb_l = swa_qkv(qb_l, kb_l, vb_l, swa_q_g, swa_k_g, rope)
    qb_c, kb_c, vb_c = swa_qkv(qb_c, kb_c, vb_c, swa_q_g, swa_k_g, None)

    def merge(ya, yb, ga, gb):
        ga = jax.nn.sigmoid(ga + b_gate[:D])
        gb = jax.nn.sigmoid(gb + b_gate[D:])
        return (ga * (ya @ w_branch_a) + gb * (yb @ w_branch_b)) @ w_out

    ya_l = diff_attn_latent(qa_l, jnp.concatenate([ka_l, ka_c], axis=1),
                            jnp.concatenate([va_l, va_c], axis=1), lam)
    ya_l = (rmsnorm(ya_l, diff_subln_g) * (1 - lam_init)).reshape(B, N, DIFF_V_W)
    yb_l = swa_latent(qb_l, kb_l, vb_l, kb_c, vb_c, swa_sink).reshape(B, N, SWA_Q_W)
    xl = xl + g1_l * merge(ya_l, yb_l, ga_l, gb_l)

    if not last:
        ya_c = (rmsnorm(diff_attend(qa_c, ka_c, va_c, lam), diff_subln_g) * (1 - lam_init)).reshape(B, L, DIFF_V_W)
        yb_c = sink_attend(qb_c, kb_c, vb_c, swa_sink).reshape(B, L, SWA_Q_W)
        xc = xc + g1_c * merge(ya_c, yb_c, ga_c, gb_c)

    hl2 = modulate(xl, norm2_g, sh2_l, s2_l)
    xl = xl + g2_l * expert_choice_ffn(hl2, w_router, w_e1, w_e3, w_e2)
    if not last:
        hc2 = modulate(xc, norm2_g, sh2_c, s2_c)
        xc = xc + g2_c * expert_choice_ffn(hc2, w_router, w_e1, w_e3, w_e2)
    return xl, xc


def setup_inputs(seed: int = 0) -> dict:
    key = jax.random.key(seed)
    ks = jax.random.split(key, 24)
    D = D_MODEL

    def nrm(k, shape, scale):
        return jax.random.normal(k, shape, F32) * scale

    return {
        "x": nrm(ks[0], (BATCH, SEQ, D), 1.0),
        "c": nrm(ks[1], (BATCH, D), 1.0),
        "ctx": nrm(ks[2], (BATCH, CTX_LEN, D), 1.0),
        "c_ctx": nrm(ks[3], (D,), 1.0),
        "w_ada": nrm(ks[4], (DEPTH, D, 6 * D), ADA_INIT_SCALE * D ** -0.5),
        "b_ada": nrm(ks[5], (DEPTH, 6 * D), 0.02),
        "norm1_g": 1 + nrm(ks[6], (DEPTH, D), 0.02),
        "w_in": nrm(ks[7], (DEPTH, D, IN_W), D ** -0.5),
        "b_gate": nrm(ks[8], (DEPTH, 2 * D), 0.02),
        "diff_q_g": 1 + nrm(ks[9], (DEPTH, HEAD_DIM), 0.02),
        "diff_k_g": 1 + nrm(ks[10], (DEPTH, HEAD_DIM), 0.02),
        "diff_lambda": nrm(ks[11], (DEPTH, 4, HEAD_DIM), 0.1),
        "diff_subln_g": 1 + nrm(ks[12], (DEPTH, DIFF_V_DIM), 0.02),
        "swa_q_g": 1 + nrm(ks[13], (DEPTH, HEAD_DIM), 0.02),
        "swa_k_g": 1 + nrm(ks[14], (DEPTH, HEAD_DIM), 0.02),
        "swa_sink": nrm(ks[15], (DEPTH, SWA_Q_HEADS), 0.5),
        "w_branch_a": nrm(ks[16], (DEPTH, DIFF_V_W, D), DIFF_V_W ** -0.5),
        "w_branch_b": nrm(ks[17], (DEPTH, SWA_Q_W, D), SWA_Q_W ** -0.5),
        "w_out": nrm(ks[18], (DEPTH, D, D), D ** -0.5),
        "norm2_g": 1 + nrm(ks[19], (DEPTH, D), 0.02),
        "w_router": nrm(ks[20], (DEPTH, D, N_EXPERTS), D ** -0.5),
        "w_e1": nrm(ks[21], (DEPTH, N_EXPERTS, D, EXPERT_FF), D ** -0.5),
        "w_e3": nrm(ks[22], (DEPTH, N_EXPERTS, D, EXPERT_FF), D ** -0.5),
        "w_e2": nrm(ks[23], (DEPTH, N_EXPERTS, EXPERT_FF, D), EXPERT_FF ** -0.5),
    }


def reference(x, c, ctx, c_ctx, w_ada, b_ada, norm1_g, w_in, b_gate, diff_q_g, diff_k_g, diff_lambda,
              diff_subln_g, swa_q_g, swa_k_g, swa_sink, w_branch_a, w_branch_b, w_out, norm2_g,
              w_router, w_e1, w_e3, w_e2):
    rope = axial_rope_tables(x.shape[1])
    sc_l = jax.nn.silu(c)
    sc_c = jax.nn.silu(c_ctx)
    xl, xc = x, ctx
    for i in range(DEPTH):
        lam_init = 0.8 - 0.6 * math.exp(-0.3 * i)
        xl, xc = hybrid_layer(
            xl, xc, sc_l, sc_c, rope, lam_init, i == DEPTH - 1,
            w_ada[i], b_ada[i], norm1_g[i], w_in[i], b_gate[i], diff_q_g[i], diff_k_g[i], diff_lambda[i],
            diff_subln_g[i], swa_q_g[i], swa_k_g[i], swa_sink[i], w_branch_a[i], w_branch_b[i], w_out[i],
            norm2_g[i], w_router[i], w_e1[i], w_e3[i], w_e2[i])
    return xl
```

```python
import functools
import math

import numpy as np
import jax
import jax.numpy as jnp
from jax import lax
from jax.experimental import pallas as pl
from jax.experimental.pallas import tpu as pltpu

F32 = jnp.float32
BF16 = jnp.bfloat16
I32 = jnp.int32

HEAD_DIM = 64
GRID_W = 64
DIFF_HEADS = 8
SWA_Q_HEADS = 16
SWA_KV_HEADS = 4
SWA_GROUP = SWA_Q_HEADS // SWA_KV_HEADS
WINDOW = 128
N_EXPERTS = 16
CAPACITY_FACTOR = 2
ROPE_THETA = 10000.0
EPS = 1e-6
NEG_INF = -1e30

LANES = 128
TOKEN_BLOCK = 256
SLOT_CHUNK = 128
PROJ_CHUNK = 512
FF_CHUNK = 1536
VMEM_LIMIT = 56 * 1024 * 1024


def _cparams(n_axes):
    return pltpu.CompilerParams(dimension_semantics=("arbitrary",) * n_axes, vmem_limit_bytes=VMEM_LIMIT)


def _ada_kernel(c_ref, w_ref, b_ref, o_ref):
    c = c_ref[...]
    sc = c * jax.nn.sigmoid(c)
    o_ref[0] = jnp.dot(sc, w_ref[0], preferred_element_type=F32) + b_ref[0]


def _ada_call(cc, w_ada, b_ada):
    depth, d, six_d = w_ada.shape
    rows = cc.shape[0]
    cols = 1536
    return pl.pallas_call(
        _ada_kernel,
        grid=(depth, six_d // cols),
        in_specs=[pl.BlockSpec((rows, d), lambda i, j: (0, 0)),
                  pl.BlockSpec((1, d, cols), lambda i, j: (i, 0, j)),
                  pl.BlockSpec((1, 1, cols), lambda i, j: (i, 0, j))],
        out_specs=pl.BlockSpec((1, rows, cols), lambda i, j: (i, 0, j)),
        out_shape=jax.ShapeDtypeStruct((depth, rows, six_d), F32),
        compiler_params=_cparams(2),
        name="ada",
    )(cc, w_ada, b_ada.reshape(depth, 1, six_d))


def _inproj_kernel(x_ref, mod_ref, g1_ref, w_ref, gm_ref, hg_ref, bg_ref, cos_ref, sin_ref,
                   qa_ref, ka_ref, va_ref, qb_ref, gate_ref, kb_ref, vb_ref, *, d):
    x = x_ref[0]
    mod = mod_ref[0, 0]
    ms = jnp.mean(x * x, axis=-1, keepdims=True)
    h = x * lax.rsqrt(ms + EPS) * g1_ref[0]
    h = (h * (1.0 + mod[1:2]) + mod[0:1]).astype(BF16)
    cos = cos_ref[...]
    sin = sin_ref[...]
    lane = lax.broadcasted_iota(I32, (1, LANES), 1)
    first_half = (lane % 32) < 16
    gm = gm_ref[...]

    def proj(c0, width):
        return jnp.dot(h, w_ref[0, :, c0:c0 + width], preferred_element_type=F32)

    def head_norm_rope(p, gain):
        w = p.shape[1]
        msq = jnp.dot((p * p).astype(BF16), gm[:w, :w], preferred_element_type=F32) * (1.0 / HEAD_DIM)
        qn = p * lax.rsqrt(msq + EPS) * gain
        outs = []
        for u in range(w // LANES):
            seg = qn[:, u * LANES:(u + 1) * LANES]
            partner = jnp.where(first_half, pltpu.roll(seg, LANES - 16, 1), pltpu.roll(seg, 16, 1))
            outs.append((seg * cos + partner * sin).astype(BF16))
        return outs

    per = PROJ_CHUNK // LANES
    for sec, (out_ref, gain_row) in enumerate(((qa_ref, 0), (ka_ref, 1))):
        for c in range(d // PROJ_CHUNK):
            c0 = sec * d + c * PROJ_CHUNK
            outs = head_norm_rope(proj(c0, PROJ_CHUNK), hg_ref[0, gain_row:gain_row + 1, c * PROJ_CHUNK:(c + 1) * PROJ_CHUNK])
            for u, o in enumerate(outs):
                out_ref[0, c * per + u] = o
    for c in range(d // PROJ_CHUNK):
        p = proj(2 * d + c * PROJ_CHUNK, PROJ_CHUNK).astype(BF16)
        for u in range(per):
            va_ref[0, c * per + u] = p[:, u * LANES:(u + 1) * LANES]
    for c in range(d // PROJ_CHUNK):
        outs = head_norm_rope(proj(3 * d + c * PROJ_CHUNK, PROJ_CHUNK), hg_ref[0, 2:3, c * PROJ_CHUNK:(c + 1) * PROJ_CHUNK])
        for u, o in enumerate(outs):
            qb_ref[0, c * per + u] = o
    for c in range(2 * d // PROJ_CHUNK):
        p = proj(4 * d + c * PROJ_CHUNK, PROJ_CHUNK) + bg_ref[0, :, c * PROJ_CHUNK:(c + 1) * PROJ_CHUNK]
        gate_ref[0, :, c * PROJ_CHUNK:(c + 1) * PROJ_CHUNK] = jax.nn.sigmoid(p).astype(BF16)
    kvw = SWA_KV_HEADS * HEAD_DIM
    outs = head_norm_rope(proj(6 * d, kvw), hg_ref[0, 3:4, :kvw])
    for u, o in enumerate(outs):
        kb_ref[0, u] = o
    p = proj(6 * d + kvw, kvw).astype(BF16)
    for u in range(kvw // LANES):
        vb_ref[0, u] = p[:, u * LANES:(u + 1) * LANES]


def _inproj_call(layer, xs, modt, norm1_g, w_in_p, gm, hg, bg, cos_t, sin_t, n_lat):
    b, t, d = xs.shape
    nt = t // TOKEN_BLOCK
    nlb = n_lat // TOKEN_BLOCK
    in_w = w_in_p.shape[-1]
    nh = d // LANES
    nkv = SWA_KV_HEADS * HEAD_DIM // LANES
    head_shape = jax.ShapeDtypeStruct((b, nh, t, LANES), BF16)
    kv_shape = jax.ShapeDtypeStruct((b, nkv, t, LANES), BF16)
    head_spec = pl.BlockSpec((1, nh, TOKEN_BLOCK, LANES), lambda bi, ti: (bi, 0, ti, 0))
    kv_spec = pl.BlockSpec((1, nkv, TOKEN_BLOCK, LANES), lambda bi, ti: (bi, 0, ti, 0))
    return pl.pallas_call(
        functools.partial(_inproj_kernel, d=d),
        grid=(b, nt),
        in_specs=[pl.BlockSpec((1, TOKEN_BLOCK, d), lambda bi, ti: (bi, ti, 0)),
                  pl.BlockSpec((1, 1, 6, d), lambda bi, ti: (bi, ti // nlb, 0, 0)),
                  pl.BlockSpec((1, 1, d), lambda bi, ti: (layer, 0, 0)),
                  pl.BlockSpec((1, d, in_w), lambda bi, ti: (layer, 0, 0)),
                  pl.BlockSpec((PROJ_CHUNK, PROJ_CHUNK), lambda bi, ti: (0, 0)),
                  pl.BlockSpec((1, 4, d), lambda bi, ti: (layer, 0, 0)),
                  pl.BlockSpec((1, 1, 2 * d), lambda bi, ti: (layer, 0, 0)),
                  pl.BlockSpec((TOKEN_BLOCK, LANES), lambda bi, ti: (ti, 0)),
                  pl.BlockSpec((TOKEN_BLOCK, LANES), lambda bi, ti: (ti, 0))],
        out_specs=[head_spec, head_spec, head_spec, head_spec,
                   pl.BlockSpec((1, TOKEN_BLOCK, 2 * d), lambda bi, ti: (bi, ti, 0)),
                   kv_spec, kv_spec],
        out_shape=[head_shape, head_shape, head_shape, head_shape,
                   jax.ShapeDtypeStruct((b, t, 2 * d), BF16), kv_shape, kv_shape],
        compiler_params=_cparams(2),
        name="inproj",
    )(xs, modt, norm1_g, w_in_p, gm, hg, bg, cos_t, sin_t)


def _diff_attn_kernel(q_ref, k_ref, v_ref, dl_ref, sg_ref, o_ref, *, n_lat, tk, lam_init):
    i = pl.program_id(2)
    is_ctx = i >= n_lat // TOKEN_BLOCK
    q = q_ref[0, 0]
    lane = lax.broadcasted_iota(I32, q.shape, 1)
    zero = jnp.zeros_like(q)
    q2 = jnp.concatenate([jnp.where(lane < HEAD_DIM, q, zero), jnp.where(lane >= HEAD_DIM, q, zero)], axis=0)
    rows = q2.shape[0]

    def step(k, v, carry):
        m, l, acc = carry
        s = lax.dot_general(q2, k, (((1,), (1,)), ((), ())), preferred_element_type=F32)
        m_new = jnp.maximum(m, jnp.max(s, axis=1, keepdims=True))
        alpha = jnp.exp(m - m_new)
        p = jnp.exp(s - m_new)
        l = alpha * l + jnp.sum(p, axis=1, keepdims=True)
        acc = alpha * acc + jnp.dot(p.astype(BF16), v, preferred_element_type=F32)
        return m_new, l, acc

    def body(j, carry):
        off = pl.multiple_of(j * tk, tk)
        return step(k_ref[0, 0, pl.ds(off, tk), :], v_ref[0, 0, pl.ds(off, tk), :], carry)

    init = (jnp.full((rows, 1), NEG_INF, F32), jnp.zeros((rows, 1), F32), jnp.zeros((rows, LANES), F32))
    carry = lax.fori_loop(0, jnp.where(is_ctx, 0, n_lat // tk), body, init)
    _, l, acc = step(k_ref[0, 0, n_lat:, :], v_ref[0, 0, n_lat:, :], carry)
    o = acc / l
    half = rows // 2
    dl = dl_ref[0]
    lam = (jnp.exp(jnp.sum(dl[0:1] * dl[1:2], axis=1, keepdims=True))
           - jnp.exp(jnp.sum(dl[2:3] * dl[3:4], axis=1, keepdims=True)) + lam_init)
    y = o[:half] - lam * o[half:]
    y = y * lax.rsqrt(jnp.mean(y * y, axis=-1, keepdims=True) + EPS) * sg_ref[0] * (1.0 - lam_init)
    o_ref[0, 0] = y.astype(BF16)


def _diff_attn_call(layer, qa, ka, va, diff_lambda, subln_g, n_lat, lam_init):
    b, nh, t, _ = qa.shape
    nt = t // TOKEN_BLOCK
    tk = 512 if n_lat % 512 == 0 else TOKEN_BLOCK
    kv_spec = pl.BlockSpec((1, 1, t, LANES), lambda bi, hi, ti: (bi, hi, 0, 0))
    q_spec = pl.BlockSpec((1, 1, TOKEN_BLOCK, LANES), lambda bi, hi, ti: (bi, hi, ti, 0))
    return pl.pallas_call(
        functools.partial(_diff_attn_kernel, n_lat=n_lat, tk=tk, lam_init=lam_init),
        grid=(b, nh, nt),
        in_specs=[q_spec, kv_spec, kv_spec,
                  pl.BlockSpec((1, 4, HEAD_DIM), lambda bi, hi, ti: (layer, 0, 0)),
                  pl.BlockSpec((1, 1, LANES), lambda bi, hi, ti: (layer, 0, 0))],
        out_specs=q_spec,
        out_shape=jax.ShapeDtypeStruct(qa.shape, BF16),
        compiler_params=_cparams(3),
        name="diff_attn",
    )(qa, ka, va, diff_lambda, subln_g)


def _swa_kernel(sink_ref, q_ref, k_ref, v_ref, o_ref, *, n_lat):
    j = pl.program_id(1)
    i = pl.program_id(2)
    is_lat = i < n_lat // TOKEN_BLOCK
    win = 2 * TOKEN_BLOCK
    start = pl.multiple_of(jnp.clip(i * TOKEN_BLOCK - WINDOW, 0, n_lat - win), WINDOW)
    kcat = jnp.concatenate([k_ref[0, 0, pl.ds(start, win), :], k_ref[0, 0, n_lat:, :]], axis=0)
    vcat = jnp.concatenate([v_ref[0, 0, pl.ds(start, win), :], v_ref[0, 0, n_lat:, :]], axis=0)
    nk = kcat.shape[0]
    qpos = i * TOKEN_BLOCK + lax.broadcasted_iota(I32, (TOKEN_BLOCK, nk), 0)
    col = lax.broadcasted_iota(I32, (TOKEN_BLOCK, nk), 1)
    dist = jnp.where(is_lat, jnp.abs(qpos - (start + col)), 2 * WINDOW)
    bias1 = jnp.where((col < win) & (dist > WINDOW), NEG_INF, 0.0).astype(F32)
    bias = jnp.concatenate([bias1] * SWA_GROUP, axis=0)
    lane = lax.broadcasted_iota(I32, (TOKEN_BLOCK, LANES), 1)
    res = []
    for half in range(2):
        in_half = (lane >= half * HEAD_DIM) & (lane < (half + 1) * HEAD_DIM)
        qs = jnp.concatenate([jnp.where(in_half, q_ref[0, g], jnp.zeros((TOKEN_BLOCK, LANES), BF16))
                              for g in range(SWA_GROUP)], axis=0)
        sink = jnp.concatenate([jnp.full((TOKEN_BLOCK, 1), sink_ref[8 * j + 4 * half + g], F32)
                                for g in range(SWA_GROUP)], axis=0)
        s = lax.dot_general(qs, kcat, (((1,), (1,)), ((), ())), preferred_element_type=F32)
        s = s + bias
        m = jnp.maximum(jnp.max(s, axis=1, keepdims=True), sink)
        p = jnp.exp(s - m)
        den = jnp.sum(p, axis=1, keepdims=True) + jnp.exp(sink - m)
        res.append(jnp.dot(p.astype(BF16), vcat, preferred_element_type=F32) / den)
    for g in range(SWA_GROUP):
        r0 = res[0][g * TOKEN_BLOCK:(g + 1) * TOKEN_BLOCK]
        r1 = res[1][g * TOKEN_BLOCK:(g + 1) * TOKEN_BLOCK]
        o_ref[0, g] = jnp.where(lane < HEAD_DIM, r0, r1).astype(BF16)


def _swa_call(layer, sink, qb, kb, vb, n_lat):
    b, nslab, t, _ = qb.shape
    nt = t // TOKEN_BLOCK
    npair = kb.shape[1]
    q_spec = pl.BlockSpec((1, SWA_GROUP, TOKEN_BLOCK, LANES), lambda bi, ji, ti, s: (bi, ji, ti, 0))
    kv_spec = pl.BlockSpec((1, 1, t, LANES), lambda bi, ji, ti, s: (bi, ji, 0, 0))
    return pl.pallas_call(
        functools.partial(_swa_kernel, n_lat=n_lat),
        grid_spec=pltpu.PrefetchScalarGridSpec(
            num_scalar_prefetch=1, grid=(b, npair, nt),
            in_specs=[q_spec, kv_spec, kv_spec], out_specs=q_spec),
        out_shape=jax.ShapeDtypeStruct(qb.shape, BF16),
        compiler_params=_cparams(3),
        name="swa_attn",
    )(sink, qb, kb, vb)


def _merge_kernel(x_ref, ya_ref, yb_ref, gate_ref, mod_ref, wa_ref, wb_ref, wo_ref, g2_ref, wr_ref, wrt_ref,
                  xo_ref, h2_ref, aff_ref, afft_ref, *, d):
    nh = ya_ref.shape[1]
    ya = jnp.concatenate([ya_ref[0, h] for h in range(nh)], axis=1)
    yb = jnp.concatenate([yb_ref[0, h] for h in range(nh)], axis=1)
    za = jnp.dot(ya, wa_ref[0], preferred_element_type=F32)
    zb = jnp.dot(yb, wb_ref[0], preferred_element_type=F32)
    gate = gate_ref[0].astype(F32)
    u = gate[:, :d] * za + gate[:, d:] * zb
    z = jnp.dot(u.astype(BF16), wo_ref[0], preferred_element_type=F32)
    mod = mod_ref[0, 0]
    xn = x_ref[0] + mod[2:3] * z
    xo_ref[0] = xn
    ms = jnp.mean(xn * xn, axis=-1, keepdims=True)
    h2 = xn * lax.rsqrt(ms + EPS) * g2_ref[0]
    h2 = (h2 * (1.0 + mod[4:5]) + mod[3:4]).astype(BF16)
    h2_ref[0] = h2
    lg = jnp.dot(h2, wr_ref[0], preferred_element_type=F32)
    e = jnp.exp(lg - jnp.max(lg, axis=1, keepdims=True))
    aff_ref[0] = e / jnp.sum(e, axis=1, keepdims=True)
    lgt = lax.dot_general(wrt_ref[0], h2, (((1,), (1,)), ((), ())), preferred_element_type=F32)
    et = jnp.exp(lgt - jnp.max(lgt, axis=0, keepdims=True))
    afft_ref[0, 0] = et / jnp.sum(et, axis=0, keepdims=True)


def _merge_call(layer, xs, ya, yb, gates, modt, wa, wb, wo, norm2_g, wr, wrt, n_lat):
    b, t, d = xs.shape
    nt = t // TOKEN_BLOCK
    nlb = n_lat // TOKEN_BLOCK
    nh = ya.shape[1]
    ne = wr.shape[-1]
    head_spec = pl.BlockSpec((1, nh, TOKEN_BLOCK, LANES), lambda bi, ti: (bi, 0, ti, 0))
    w_spec = pl.BlockSpec((1, d, d), lambda bi, ti: (layer, 0, 0))
    tok_spec = pl.BlockSpec((1, TOKEN_BLOCK, d), lambda bi, ti: (bi, ti, 0))
    return pl.pallas_call(
        functools.partial(_merge_kernel, d=d),
        grid=(b, nt),
        in_specs=[tok_spec, head_spec, head_spec,
                  pl.BlockSpec((1, TOKEN_BLOCK, 2 * d), lambda bi, ti: (bi, ti, 0)),
                  pl.BlockSpec((1, 1, 6, d), lambda bi, ti: (bi, ti // nlb, 0, 0)),
                  w_spec, w_spec, w_spec,
                  pl.BlockSpec((1, 1, d), lambda bi, ti: (layer, 0, 0)),
                  pl.BlockSpec((1, d, ne), lambda bi, ti: (layer, 0, 0)),
                  pl.BlockSpec((1, ne, d), lambda bi, ti: (layer, 0, 0))],
        out_specs=[tok_spec, tok_spec,
                   pl.BlockSpec((1, TOKEN_BLOCK, ne), lambda bi, ti: (bi, ti, 0)),
                   pl.BlockSpec((1, 1, ne, TOKEN_BLOCK), lambda bi, ti: (bi, ti, 0, 0))],
        out_shape=[jax.ShapeDtypeStruct((b, t, d), F32), jax.ShapeDtypeStruct((b, t, d), BF16),
                   jax.ShapeDtypeStruct((b, t, ne), F32), jax.ShapeDtypeStruct((b, nt, ne, TOKEN_BLOCK), F32)],
        compiler_params=_cparams(2),
        name="merge",
    )(xs, ya, yb, gates, modt, wa, wb, wo, norm2_g, wr, wrt)


def _route_kernel(afft_ref, tri_ref, pos_ref, cnt_ref, *, nlb, cap_l, cap_c):
    nt = afft_ref.shape[1]
    ne = afft_ref.shape[2]
    tri = tri_ref[...]

    def bits(blk):
        return lax.bitcast_convert_type(afft_ref[0, blk], I32)

    def count(pred_fn, blocks):
        acc = jnp.zeros((ne, TOKEN_BLOCK), F32)
        for blk in blocks:
            acc = acc + jnp.where(pred_fn(bits(blk)), 1.0, 0.0)
        return jnp.sum(acc, axis=1, keepdims=True)

    def select(blocks, cap, base, with_counts):
        def it(k, thr):
            cand = thr | jnp.left_shift(jnp.int32(1), 30 - k)
            return jnp.where(count(lambda bt: bt >= cand, blocks) >= cap, cand, thr)
        thr = lax.fori_loop(0, 31, it, jnp.zeros((ne, 1), I32))
        ties_wanted = cap - count(lambda bt: bt > thr, blocks)
        seen_eq = jnp.zeros((ne, 1), F32)
        seen_sel = jnp.zeros((ne, 1), F32)
        lane = lax.broadcasted_iota(I32, (ne, LANES), 1)
        cntv = jnp.zeros((ne, LANES), F32)
        for n, blk in enumerate(blocks):
            bt = bits(blk)
            eq = bt == thr
            eqf = jnp.where(eq, 1.0, 0.0)
            rank = jnp.dot(eqf.astype(BF16), tri, preferred_element_type=F32) + seen_eq
            self_ = jnp.where(eq, jnp.where(rank < ties_wanted, 1.0, 0.0), jnp.where(bt > thr, 1.0, 0.0))
            sel = self_ > 0.5
            slot = jnp.dot(self_.astype(BF16), tri, preferred_element_type=F32) + seen_sel
            pos_ref[0, blk] = jnp.where(sel, slot.astype(I32) + base, -1)
            if with_counts:
                cntv = jnp.where(lane == n, seen_sel, cntv)
            seen_eq = seen_eq + jnp.sum(eqf, axis=1, keepdims=True)
            seen_sel = seen_sel + jnp.sum(self_, axis=1, keepdims=True)
        if with_counts:
            cntv = jnp.where(lane == len(blocks), seen_sel, cntv)
            cnt_ref[0] = cntv.astype(I32)

    select(list(range(nlb)), cap_l, 0, True)
    select(list(range(nlb, nt)), cap_c, cap_l, False)


def _route_call(afft, tri, n_lat, cap_l, cap_c):
    b, nt, ne, _ = afft.shape
    return pl.pallas_call(
        functools.partial(_route_kernel, nlb=n_lat // TOKEN_BLOCK, cap_l=cap_l, cap_c=cap_c),
        grid=(b,),
        in_specs=[pl.BlockSpec((1, nt, ne, TOKEN_BLOCK), lambda bi: (bi, 0, 0, 0)),
                  pl.BlockSpec((TOKEN_BLOCK, TOKEN_BLOCK), lambda bi: (0, 0))],
        out_specs=[pl.BlockSpec((1, nt, ne, TOKEN_BLOCK), lambda bi: (bi, 0, 0, 0)),
                   pl.BlockSpec((1, ne, LANES), lambda bi: (bi, 0, 0))],
        out_shape=[jax.ShapeDtypeStruct(afft.shape, I32), jax.ShapeDtypeStruct((b, ne, LANES), I32)],
        compiler_params=_cparams(1),
        name="route",
    )(afft, tri)


def _slot_onehot(pos_row, first_slot, n_slots):
    slots = first_slot + lax.broadcasted_iota(I32, (n_slots, pos_row.shape[1]), 0)
    return jnp.where(pos_row == slots, 1.0, 0.0).astype(BF16)


def _gather_kernel(cnt_ref, h2_ref, pos_ref, xe_ref, acc_ref, *, nlb, cap_l, cap_c):
    bi = pl.program_id(0)
    e = pl.program_id(1)
    ne = pl.num_programs(1)
    cbase = (bi * ne + e) * (nlb + 1)
    for c in range(cap_l // SLOT_CHUNK):
        s0 = c * SLOT_CHUNK
        acc_ref[...] = jnp.zeros_like(acc_ref)

        def body(blk, _, s0=s0):
            lo = cnt_ref[cbase + blk]
            hi = cnt_ref[cbase + blk + 1]

            @pl.when((hi > s0) & (lo < s0 + SLOT_CHUNK))
            def _():
                off = pl.multiple_of(blk * TOKEN_BLOCK, TOKEN_BLOCK)
                onehot = _slot_onehot(pos_ref[0, blk, pl.ds(e, 1), :], s0, SLOT_CHUNK)
                acc_ref[...] += jnp.dot(onehot, h2_ref[0, pl.ds(off, TOKEN_BLOCK), :],
                                        preferred_element_type=F32)
            return 0

        lax.fori_loop(0, nlb, body, 0)
        xe_ref[0, 0, s0:s0 + SLOT_CHUNK, :] = acc_ref[...].astype(BF16)
    ctx_acc = jnp.zeros((cap_c, h2_ref.shape[2]), F32)
    for blk in range(nlb, pos_ref.shape[1]):
        onehot = _slot_onehot(pos_ref[0, blk, pl.ds(e, 1), :], cap_l, cap_c)
        t0 = blk * TOKEN_BLOCK
        ctx_acc = ctx_acc + jnp.dot(onehot, h2_ref[0, t0:t0 + TOKEN_BLOCK, :], preferred_element_type=F32)
    xe_ref[0, 0, cap_l:cap_l + cap_c, :] = ctx_acc.astype(BF16)


def _gather_call(cnt_flat, h2, pos, n_lat, cap_l, cap_c):
    b, t, d = h2.shape
    _, nt, ne, _ = pos.shape
    slots = cap_l + cap_c
    return pl.pallas_call(
        functools.partial(_gather_kernel, nlb=n_lat // TOKEN_BLOCK, cap_l=cap_l, cap_c=cap_c),
        grid_spec=pltpu.PrefetchScalarGridSpec(
            num_scalar_prefetch=1, grid=(b, ne),
            in_specs=[pl.BlockSpec((1, t, d), lambda bi, ei, s: (bi, 0, 0)),
                      pl.BlockSpec((1, nt, ne, TOKEN_BLOCK), lambda bi, ei, s: (bi, 0, 0, 0))],
            out_specs=pl.BlockSpec((1, 1, slots, d), lambda bi, ei, s: (bi, ei, 0, 0)),
            scratch_shapes=[pltpu.VMEM((SLOT_CHUNK, d), F32)]),
        out_shape=jax.ShapeDtypeStruct((b, ne, slots, d), BF16),
        compiler_params=_cparams(2),
        name="moe_gather",
    )(cnt_flat, h2, pos)


def _ffn_kernel(x_ref, w1_ref, w3_ref, w2_ref, o_ref):
    x = x_ref[0, 0]
    ff = w1_ref.shape[-1]
    acc = None
    for f0 in range(0, ff, FF_CHUNK):
        f1 = min(ff, f0 + FF_CHUNK)
        a = jnp.dot(x, w1_ref[0, 0, :, f0:f1], preferred_element_type=F32)
        g = jnp.dot(x, w3_ref[0, 0, :, f0:f1], preferred_element_type=F32)
        hid = (a * jax.nn.sigmoid(a) * g).astype(BF16)
        part = jnp.dot(hid, w2_ref[0, 0, f0:f1, :], preferred_element_type=F32)
        acc = part if acc is None else acc + part
    o_ref[0, 0] = acc.astype(BF16)


def _ffn_call(layer, xe, w1, w3, w2):
    b, ne, slots, d = xe.shape
    ff = w1.shape[-1]
    x_spec = pl.BlockSpec((1, 1, slots, d), lambda ei, bi: (bi, ei, 0, 0))
    return pl.pallas_call(
        _ffn_kernel,
        grid=(ne, b),
        in_specs=[x_spec,
                  pl.BlockSpec((1, 1, d, ff), lambda ei, bi: (layer, ei, 0, 0)),
                  pl.BlockSpec((1, 1, d, ff), lambda ei, bi: (layer, ei, 0, 0)),
                  pl.BlockSpec((1, 1, ff, d), lambda ei, bi: (layer, ei, 0, 0))],
        out_specs=x_spec,
        out_shape=jax.ShapeDtypeStruct(xe.shape, BF16),
        compiler_params=_cparams(2),
        name="moe_ffn",
    )(xe, w1, w3, w2)


def _combine_kernel(cnt_ref, x_ref, ye_ref, pos_ref, aff_ref, mod_ref, xo_ref, acc_ref, *, nlb, cap_l, cap_c):
    bi = pl.program_id(0)
    ti = pl.program_id(1)
    ne = ye_ref.shape[1]
    is_lat = ti < nlb
    acc_ref[...] = jnp.zeros_like(acc_ref)
    tn = (((0,), (0,)), ((), ()))
    for e in range(ne):
        gate = aff_ref[0, :, e:e + 1]
        pos_row = pos_ref[0, 0, e:e + 1, :]

        @pl.when(is_lat)
        def _(e=e, gate=gate, pos_row=pos_row):
            tb = jnp.minimum(ti, nlb - 1)
            cbase = (bi * ne + e) * (nlb + 1) + tb
            lo = cnt_ref[cbase]
            hi = cnt_ref[cbase + 1]

            def body(c, _):
                s0 = pl.multiple_of(c * SLOT_CHUNK, SLOT_CHUNK)
                onehot = _slot_onehot(pos_row, s0, SLOT_CHUNK)
                contrib = lax.dot_general(onehot, ye_ref[0, e, pl.ds(s0, SLOT_CHUNK), :], tn,
                                          preferred_element_type=F32)
                acc_ref[...] += gate * contrib
                return 0

            lax.fori_loop(lo // SLOT_CHUNK, jnp.where(hi > lo, (hi + SLOT_CHUNK - 1) // SLOT_CHUNK, 0), body, 0)

        @pl.when(jnp.logical_not(is_lat))
        def _(e=e, gate=gate, pos_row=pos_row):
            onehot = _slot_onehot(pos_row, cap_l, cap_c)
            contrib = lax.dot_general(onehot, ye_ref[0, e, cap_l:cap_l + cap_c, :], tn,
                                      preferred_element_type=F32)
            acc_ref[...] += gate * contrib

    xo_ref[0] = x_ref[0] + mod_ref[0, 0][5:6] * acc_ref[...]


def _combine_call(cnt_flat, xs, ye, pos, aff, modt, n_lat, cap_l, cap_c):
    b, t, d = xs.shape
    _, nt, ne, _ = pos.shape
    nlb = n_lat // TOKEN_BLOCK
    slots = cap_l + cap_c
    tok_spec = pl.BlockSpec((1, TOKEN_BLOCK, d), lambda bi, ti, s: (bi, ti, 0))
    return pl.pallas_call(
        functools.partial(_combine_kernel, nlb=nlb, cap_l=cap_l, cap_c=cap_c),
        grid_spec=pltpu.PrefetchScalarGridSpec(
            num_scalar_prefetch=1, grid=(b, nt),
            in_specs=[tok_spec,
                      pl.BlockSpec((1, ne, slots, d), lambda bi, ti, s: (bi, 0, 0, 0)),
                      pl.BlockSpec((1, 1, ne, TOKEN_BLOCK), lambda bi, ti, s: (bi, ti, 0, 0)),
                      pl.BlockSpec((1, TOKEN_BLOCK, ne), lambda bi, ti, s: (bi, ti, 0)),
                      pl.BlockSpec((1, 1, 6, d), lambda bi, ti, s: (bi, ti // nlb, 0, 0))],
            out_specs=tok_spec,
            scratch_shapes=[pltpu.VMEM((TOKEN_BLOCK, d), F32)]),
        out_shape=jax.ShapeDtypeStruct(xs.shape, F32),
        compiler_params=_cparams(2),
        name="moe_combine",
    )(cnt_flat, xs, ye, pos, aff, modt)


def _rope_tables(n_lat, n_ctx):
    rows = n_lat // GRID_W
    row = jnp.repeat(jnp.arange(rows), GRID_W).astype(F32)
    col = jnp.tile(jnp.arange(GRID_W), rows).astype(F32)
    half = HEAD_DIM // 2
    inv_freq = ROPE_THETA ** (-jnp.arange(0, half, 2, dtype=F32) / half)
    ar = row[:, None] * inv_freq
    ac = col[:, None] * inv_freq
    cos64 = jnp.concatenate([jnp.cos(ar), jnp.cos(ar), jnp.cos(ac), jnp.cos(ac)], axis=1)
    sin64 = jnp.concatenate([-jnp.sin(ar), jnp.sin(ar), -jnp.sin(ac), jnp.sin(ac)], axis=1)
    cos_t = jnp.concatenate([jnp.tile(cos64, (1, LANES // HEAD_DIM)), jnp.ones((n_ctx, LANES), F32)], axis=0)
    sin_t = jnp.concatenate([jnp.tile(sin64, (1, LANES // HEAD_DIM)), jnp.zeros((n_ctx, LANES), F32)], axis=0)
    return cos_t, sin_t


def _swa_head_order():
    heads = []
    for j in range(SWA_KV_HEADS // 2):
        for g in range(SWA_GROUP):
            heads += [2 * SWA_GROUP * j + g, 2 * SWA_GROUP * j + SWA_GROUP + g]
    return np.concatenate([np.arange(h * HEAD_DIM, (h + 1) * HEAD_DIM) for h in heads])


def kernel(x, c, ctx, c_ctx, w_ada, b_ada, norm1_g, w_in, b_gate, diff_q_g, diff_k_g, diff_lambda, diff_subln_g,
           swa_q_g, swa_k_g, swa_sink, w_branch_a, w_branch_b, w_out, norm2_g, w_router, w_e1, w_e3, w_e2):
    b, n_lat, d = x.shape
    n_ctx = ctx.shape[1]
    depth = w_ada.shape[0]
    assert n_lat % TOKEN_BLOCK == 0 and n_ctx % TOKEN_BLOCK == 0 and n_lat >= 2 * TOKEN_BLOCK
    assert d == DIFF_HEADS * 2 * HEAD_DIM == SWA_Q_HEADS * HEAD_DIM and w_router.shape[-1] == N_EXPERTS
    cap_l = CAPACITY_FACTOR * n_lat // N_EXPERTS
    cap_c = CAPACITY_FACTOR * n_ctx // N_EXPERTS
    assert cap_l % SLOT_CHUNK == 0 and cap_c % 16 == 0
    nlb = n_lat // TOKEN_BLOCK

    perm = _swa_head_order()
    kvw = SWA_KV_HEADS * HEAD_DIM
    o = np.cumsum([0, d, d, d, d, kvw, kvw, d, d])
    w_in_p = jnp.concatenate(
        [w_in[..., o[0]:o[3]], w_in[..., o[3]:o[4]][..., perm], w_in[..., o[6]:o[8]], w_in[..., o[4]:o[6]]],
        axis=-1).astype(BF16)
    wa = w_branch_a.astype(BF16)
    wb = w_branch_b[:, perm, :].astype(BF16)
    wo = w_out.astype(BF16)
    wr = w_router.astype(BF16)
    wrt = jnp.swapaxes(w_router, 1, 2).astype(BF16)
    w1 = w_e1.astype(BF16)
    w3 = w_e3.astype(BF16)
    w2 = w_e2.astype(BF16)
    scale = HEAD_DIM ** -0.5
    hg = jnp.stack([jnp.tile(diff_q_g, (1, d // HEAD_DIM)) * scale,
                    jnp.tile(diff_k_g, (1, d // HEAD_DIM)),
                    jnp.tile(swa_q_g, (1, d // HEAD_DIM)) * scale,
                    jnp.tile(swa_k_g, (1, d // HEAD_DIM))], axis=1)
    subln = diff_subln_g.reshape(depth, 1, LANES)
    g1n = norm1_g.reshape(depth, 1, d)
    g2n = norm2_g.reshape(depth, 1, d)
    bg = b_gate.reshape(depth, 1, 2 * d)
    gm = jnp.asarray(np.kron(np.eye(PROJ_CHUNK // HEAD_DIM), np.ones((HEAD_DIM, HEAD_DIM))), BF16)
    tri = jnp.asarray(np.triu(np.ones((TOKEN_BLOCK, TOKEN_BLOCK)), 1), BF16)
    cos_t, sin_t = _rope_tables(n_lat, n_ctx)

    rows = -(-(b + 1) // 8) * 8
    cc = jnp.concatenate([c, c_ctx[None], jnp.zeros((rows - b - 1, d), F32)], axis=0)
    mod_all = _ada_call(cc, w_ada, b_ada)

    xs = jnp.concatenate([x, ctx], axis=1)
    for i in range(depth):
        lam_init = 0.8 - 0.6 * math.exp(-0.3 * i)
        mod_l = mod_all[i, :b].reshape(b, 1, 6, d)
        mod_c = jnp.broadcast_to(mod_all[i, b].reshape(1, 1, 6, d), (b, 1, 6, d))
        modt = jnp.concatenate([mod_l, mod_c], axis=1)
        qa, ka, va, qb, gates, kb, vb = _inproj_call(i, xs, modt, g1n, w_in_p, gm, hg, bg, cos_t, sin_t, n_lat)
        ya = _diff_attn_call(i, qa, ka, va, diff_lambda, subln, n_lat, lam_init)
        yb = _swa_call(i, swa_sink[i], qb, kb, vb, n_lat)
        xs, h2, aff, afft = _merge_call(i, xs, ya, yb, gates, modt, wa, wb, wo, g2n, wr, wrt, n_lat)
        pos, cnt = _route_call(afft, tri, n_lat, cap_l, cap_c)
        cnt_flat = cnt[:, :, :nlb + 1].reshape(-1)
        xe = _gather_call(cnt_flat, h2, pos, n_lat, cap_l, cap_c)
        ye = _ffn_call(i, xe, w1, w3, w2)
        xs = _combine_call(cnt_flat, xs, ye, pos, aff, modt, n_lat, cap_l, cap_c)
    return xs[:, :n_lat]
```

```python
import functools
import math

import numpy as np
import jax
import jax.numpy as jnp
from jax import lax
from jax.experimental import pallas as pl
from jax.experimental.pallas import tpu as pltpu

F32 = jnp.float32
BF16 = jnp.bfloat16
I32 = jnp.int32

HEAD_DIM = 64
GRID_W = 64
DIFF_HEADS = 8
SWA_Q_HEADS = 16
SWA_KV_HEADS = 4
SWA_GROUP = SWA_Q_HEADS // SWA_KV_HEADS
WINDOW = 128
N_EXPERTS = 16
CAPACITY_FACTOR = 2
ROPE_THETA = 10000.0
EPS = 1e-6
NEG_INF = -1e30

LANES = 128
TOKEN_BLOCK = 256
SLOT_CHUNK = 128
BF16_ROWS = 16
GATHER_WINDOW = 64
PROJ_CHUNK = 512
FF_CHUNK = 1536
VT_ROWS = LANES + 16
VMEM_LIMIT = 56 * 1024 * 1024


def _cparams(n_axes):
    return pltpu.CompilerParams(dimension_semantics=("arbitrary",) * n_axes, vmem_limit_bytes=VMEM_LIMIT)


def _ada_kernel(c_ref, w_ref, b_ref, o_ref):
    c = c_ref[...]
    sc = c * jax.nn.sigmoid(c)
    o_ref[0] = jnp.dot(sc, w_ref[0], preferred_element_type=F32) + b_ref[0]


def _ada_call(cc, w_ada, b_ada):
    depth, d, six_d = w_ada.shape
    rows = cc.shape[0]
    cols = 1536
    return pl.pallas_call(
        _ada_kernel,
        grid=(depth, six_d // cols),
        in_specs=[pl.BlockSpec((rows, d), lambda i, j: (0, 0)),
                  pl.BlockSpec((1, d, cols), lambda i, j: (i, 0, j)),
                  pl.BlockSpec((1, 1, cols), lambda i, j: (i, 0, j))],
        out_specs=pl.BlockSpec((1, rows, cols), lambda i, j: (i, 0, j)),
        out_shape=jax.ShapeDtypeStruct((depth, rows, six_d), F32),
        compiler_params=_cparams(2),
        name="ada",
    )(cc, w_ada, b_ada.reshape(depth, 1, six_d))


def _inproj_kernel(x_ref, mod_ref, g1_ref, w_ref, gm_ref, hg_ref, bg_ref, cos_ref, sin_ref,
                   qa_ref, ka_ref, va_ref, qb_ref, gate_ref, kb_ref, vb_ref, *, d):
    x = x_ref[0]
    mod = mod_ref[0, 0]
    ms = jnp.mean(x * x, axis=-1, keepdims=True)
    h = x * lax.rsqrt(ms + EPS) * g1_ref[0]
    h = (h * (1.0 + mod[1:2]) + mod[0:1]).astype(BF16)
    cos = cos_ref[...]
    sin = sin_ref[...]
    lane = lax.broadcasted_iota(I32, (1, LANES), 1)
    first_half = (lane % 32) < 16
    gm = gm_ref[...]

    def proj(c0, width):
        return jnp.dot(h, w_ref[0, :, c0:c0 + width], preferred_element_type=F32)

    def head_norm_rope(p, gain):
        w = p.shape[1]
        gw = gm.shape[0]
        sq = (p * p).astype(BF16)
        msq = jnp.concatenate([jnp.dot(sq[:, g0:g0 + gw], gm, preferred_element_type=F32)
                               for g0 in range(0, w, gw)], axis=1) * (1.0 / HEAD_DIM)
        qn = p * lax.rsqrt(msq + EPS) * gain
        outs = []
        for u in range(w // LANES):
            seg = qn[:, u * LANES:(u + 1) * LANES]
            partner = jnp.where(first_half, pltpu.roll(seg, LANES - 16, 1), pltpu.roll(seg, 16, 1))
            outs.append((seg * cos + partner * sin).astype(BF16))
        return outs

    per = PROJ_CHUNK // LANES
    for sec, (out_ref, gain_row) in enumerate(((qa_ref, 0), (ka_ref, 1))):
        for c in range(d // PROJ_CHUNK):
            c0 = sec * d + c * PROJ_CHUNK
            outs = head_norm_rope(proj(c0, PROJ_CHUNK), hg_ref[0, gain_row:gain_row + 1, c * PROJ_CHUNK:(c + 1) * PROJ_CHUNK])
            for u, o in enumerate(outs):
                out_ref[0, c * per + u] = o
    for c in range(d // PROJ_CHUNK):
        p = proj(2 * d + c * PROJ_CHUNK, PROJ_CHUNK)
        for u in range(per):
            va_ref[0, c * per + u, 0, :LANES, :] = p[:, u * LANES:(u + 1) * LANES].T.astype(BF16)
            va_ref[0, c * per + u, 0, LANES:, :] = jnp.ones((VT_ROWS - LANES, TOKEN_BLOCK), BF16)
    for c in range(d // PROJ_CHUNK):
        outs = head_norm_rope(proj(3 * d + c * PROJ_CHUNK, PROJ_CHUNK), hg_ref[0, 2:3, c * PROJ_CHUNK:(c + 1) * PROJ_CHUNK])
        for u, o in enumerate(outs):
            qb_ref[0, c * per + u] = o
    for c in range(2 * d // PROJ_CHUNK):
        p = proj(4 * d + c * PROJ_CHUNK, PROJ_CHUNK) + bg_ref[0, :, c * PROJ_CHUNK:(c + 1) * PROJ_CHUNK]
        gate_ref[0, :, c * PROJ_CHUNK:(c + 1) * PROJ_CHUNK] = jax.nn.sigmoid(p).astype(BF16)
    kvw = SWA_KV_HEADS * HEAD_DIM
    outs = head_norm_rope(proj(6 * d, kvw), hg_ref[0, 3:4, :kvw])
    for u, o in enumerate(outs):
        kb_ref[0, u] = o
    p = proj(6 * d + kvw, kvw)
    for u in range(kvw // LANES):
        vb_ref[0, u, 0, :LANES, :] = p[:, u * LANES:(u + 1) * LANES].T.astype(BF16)
        vb_ref[0, u, 0, LANES:, :] = jnp.ones((VT_ROWS - LANES, TOKEN_BLOCK), BF16)


def _inproj_call(layer, xs, modt, norm1_g, w_in_p, gm, hg, bg, cos_t, sin_t, n_lat):
    b, t, d = xs.shape
    nt = t // TOKEN_BLOCK
    nlb = n_lat // TOKEN_BLOCK
    in_w = w_in_p.shape[-1]
    nh = d // LANES
    nkv = SWA_KV_HEADS * HEAD_DIM // LANES
    head_shape = jax.ShapeDtypeStruct((b, nh, t, LANES), BF16)
    kv_shape = jax.ShapeDtypeStruct((b, nkv, t, LANES), BF16)
    head_spec = pl.BlockSpec((1, nh, TOKEN_BLOCK, LANES), lambda bi, ti: (bi, 0, ti, 0))
    kv_spec = pl.BlockSpec((1, nkv, TOKEN_BLOCK, LANES), lambda bi, ti: (bi, 0, ti, 0))
    vt_shape = jax.ShapeDtypeStruct((b, nh, nt, VT_ROWS, TOKEN_BLOCK), BF16)
    vt_spec = pl.BlockSpec((1, nh, 1, VT_ROWS, TOKEN_BLOCK), lambda bi, ti: (bi, 0, ti, 0, 0))
    kvt_shape = jax.ShapeDtypeStruct((b, nkv, nt, VT_ROWS, TOKEN_BLOCK), BF16)
    kvt_spec = pl.BlockSpec((1, nkv, 1, VT_ROWS, TOKEN_BLOCK), lambda bi, ti: (bi, 0, ti, 0, 0))
    return pl.pallas_call(
        functools.partial(_inproj_kernel, d=d),
        grid=(b, nt),
        in_specs=[pl.BlockSpec((1, TOKEN_BLOCK, d), lambda bi, ti: (bi, ti, 0)),
                  pl.BlockSpec((1, 1, 6, d), lambda bi, ti: (bi, ti // nlb, 0, 0)),
                  pl.BlockSpec((1, 1, d), lambda bi, ti: (layer, 0, 0)),
                  pl.BlockSpec((1, d, in_w), lambda bi, ti: (layer, 0, 0)),
                  pl.BlockSpec(gm.shape, lambda bi, ti: (0, 0)),
                  pl.BlockSpec((1, 4, d), lambda bi, ti: (layer, 0, 0)),
                  pl.BlockSpec((1, 1, 2 * d), lambda bi, ti: (layer, 0, 0)),
                  pl.BlockSpec((TOKEN_BLOCK, LANES), lambda bi, ti: (ti, 0)),
                  pl.BlockSpec((TOKEN_BLOCK, LANES), lambda bi, ti: (ti, 0))],
        out_specs=[head_spec, head_spec, vt_spec, head_spec,
                   pl.BlockSpec((1, TOKEN_BLOCK, 2 * d), lambda bi, ti: (bi, ti, 0)),
                   kv_spec, kvt_spec],
        out_shape=[head_shape, head_shape, vt_shape, head_shape,
                   jax.ShapeDtypeStruct((b, t, 2 * d), BF16), kv_shape, kvt_shape],
        compiler_params=_cparams(2),
        name="inproj",
    )(xs, modt, norm1_g, w_in_p, gm, hg, bg, cos_t, sin_t)


def _diff_attn_kernel(q_ref, k_ref, v_ref, dl_ref, sg_ref, o_ref, *, nlb, tk, lam_init):
    i = pl.program_id(2)
    is_ctx = i >= nlb
    nt = v_ref.shape[2]
    q = q_ref[0, 0]
    lane = lax.broadcasted_iota(I32, q.shape, 1)
    zero = jnp.zeros_like(q)
    q2 = jnp.concatenate([jnp.where(lane < HEAD_DIM, q, zero), jnp.where(lane >= HEAD_DIM, q, zero)], axis=0)
    nq = q2.shape[0]
    nn_t = (((1,), (1,)), ((), ()))
    bpu = tk // TOKEN_BLOCK

    def scores(key_off, n):
        k = k_ref[0, 0, pl.ds(key_off, n), :]
        return lax.dot_general(k, q2, nn_t, preferred_element_type=F32).astype(BF16)

    def values(blk0, nblk):
        return jnp.concatenate([v_ref[0, 0, blk0 + u] for u in range(nblk)], axis=1)

    def process(s, vt, state):
        m, l, acc = state
        m_new = jnp.maximum(m, jnp.max(s, axis=0, keepdims=True).astype(F32))
        alpha = jnp.exp2(m - m_new)
        p = jnp.exp2(s - m_new.astype(BF16))
        r = jnp.dot(vt, p, preferred_element_type=F32)
        return m_new, alpha * l + r[LANES:LANES + 1], alpha * acc + r[:LANES]

    init = (jnp.full((1, nq), NEG_INF, F32), jnp.zeros((1, nq), F32), jnp.zeros((LANES, nq), F32))

    def finish(state):
        _, l, acc = state
        o = acc / l
        dl = dl_ref[0]
        lam = (jnp.exp(jnp.sum(dl[0:1] * dl[1:2], axis=1, keepdims=True))
               - jnp.exp(jnp.sum(dl[2:3] * dl[3:4], axis=1, keepdims=True)) + lam_init)
        y = (o[:, :nq // 2] - lam * o[:, nq // 2:]).T
        y = y * lax.rsqrt(jnp.mean(y * y, axis=-1, keepdims=True) + EPS) * sg_ref[0] * (1.0 - lam_init)
        o_ref[0, 0] = y.astype(BF16)

    def ctx_scores():
        return scores(nlb * TOKEN_BLOCK, (nt - nlb) * TOKEN_BLOCK)

    @pl.when(is_ctx)
    def _():
        finish(process(ctx_scores(), values(nlb, nt - nlb), init))

    @pl.when(jnp.logical_not(is_ctx))
    def _():
        n_units = nlb // bpu
        state = init
        s_cur = scores(0, tk)
        for u in range(n_units):
            s_next = scores((u + 1) * tk, tk) if u + 1 < n_units else ctx_scores()
            state = process(s_cur, values(u * bpu, bpu), state)
            s_cur = s_next
        finish(process(s_cur, values(nlb, nt - nlb), state))


def _diff_attn_call(layer, qa, ka, vat, diff_lambda, subln_g, n_lat, lam_init):
    b, nh, t, _ = qa.shape
    nt = t // TOKEN_BLOCK
    nlb = n_lat // TOKEN_BLOCK
    tk = next(u for u in (1024, 512, 256) if n_lat % u == 0)
    k_spec = pl.BlockSpec((1, 1, t, LANES), lambda bi, hi, ti: (bi, hi, 0, 0))
    v_spec = pl.BlockSpec((1, 1, nt, VT_ROWS, TOKEN_BLOCK), lambda bi, hi, ti: (bi, hi, 0, 0, 0))
    q_spec = pl.BlockSpec((1, 1, TOKEN_BLOCK, LANES), lambda bi, hi, ti: (bi, hi, ti, 0))
    return pl.pallas_call(
        functools.partial(_diff_attn_kernel, nlb=nlb, tk=tk, lam_init=lam_init),
        grid=(b, nh, nt),
        in_specs=[q_spec, k_spec, v_spec,
                  pl.BlockSpec((1, 4, HEAD_DIM), lambda bi, hi, ti: (layer, 0, 0)),
                  pl.BlockSpec((1, 1, LANES), lambda bi, hi, ti: (layer, 0, 0))],
        out_specs=q_spec,
        out_shape=jax.ShapeDtypeStruct(qa.shape, BF16),
        compiler_params=_cparams(3),
        name="diff_attn",
    )(qa, ka, vat, diff_lambda, subln_g)


def _swa_kernel(sink_ref, q_ref, k_ref, v_ref, band_ref, o_ref, *, nlb):
    j = pl.program_id(1)
    i = pl.program_id(2)
    nt = v_ref.shape[2]
    n_lat = nlb * TOKEN_BLOCK
    is_lat = i < nlb
    im1 = jnp.maximum(i - 1, 0)
    ip1 = jnp.minimum(i + 1, nt - 1)

    def krows(blk, lo, n):
        return k_ref[0, 0, pl.ds(pl.multiple_of(blk * TOKEN_BLOCK + lo, WINDOW), n), :]

    kcat = jnp.concatenate([krows(im1, WINDOW, WINDOW), krows(i, 0, TOKEN_BLOCK), krows(ip1, 0, WINDOW),
                            k_ref[0, 0, n_lat:, :]], axis=0)
    vcat = jnp.concatenate([v_ref[0, 0, im1][:, WINDOW:], v_ref[0, 0, i], v_ref[0, 0, ip1][:, :WINDOW]]
                           + [v_ref[0, 0, blk] for blk in range(nlb, nt)], axis=1)
    nk = kcat.shape[0]
    neg = jnp.float32(NEG_INF)
    zero = jnp.float32(0.0)
    pen = (jnp.where((i == 0) | jnp.logical_not(is_lat), neg, zero), jnp.where(is_lat, zero, neg),
           jnp.where(i >= nlb - 1, neg, zero))
    part = jnp.concatenate([jnp.full((WINDOW, TOKEN_BLOCK), pen[0], F32), jnp.full((TOKEN_BLOCK, TOKEN_BLOCK), pen[1], F32),
                            jnp.full((WINDOW, TOKEN_BLOCK), pen[2], F32),
                            jnp.zeros((nk - 2 * TOKEN_BLOCK, TOKEN_BLOCK), F32)], axis=0)
    bias1 = band_ref[...] + part.astype(BF16)
    bias = jnp.concatenate([bias1] * SWA_GROUP, axis=1)
    lane = lax.broadcasted_iota(I32, (TOKEN_BLOCK, LANES), 1)
    nn_t = (((1,), (1,)), ((), ()))
    log2e = math.log2(math.e)

    def scores(half):
        in_half = (lane >= half * HEAD_DIM) & (lane < (half + 1) * HEAD_DIM)
        qs = jnp.concatenate([jnp.where(in_half, q_ref[0, g], jnp.zeros((TOKEN_BLOCK, LANES), BF16))
                              for g in range(SWA_GROUP)], axis=0)
        return lax.dot_general(kcat, qs, nn_t, preferred_element_type=F32).astype(BF16) + bias

    def attend(s, half):
        sink = jnp.concatenate(
            [jnp.full((1, TOKEN_BLOCK), sink_ref[2 * SWA_GROUP * j + SWA_GROUP * half + g] * log2e, F32)
             for g in range(SWA_GROUP)], axis=1)
        m = jnp.maximum(jnp.max(s, axis=0, keepdims=True).astype(F32), sink).astype(BF16)
        p = jnp.exp2(s - m)
        r = jnp.dot(vcat, p, preferred_element_type=F32)
        return r[:LANES] / (r[LANES:LANES + 1] + jnp.exp2(sink - m.astype(F32)))

    s0 = scores(0)
    s1 = scores(1)
    o0 = attend(s0, 0)
    o1 = attend(s1, 1)
    row = lax.broadcasted_iota(I32, (LANES, TOKEN_BLOCK), 0)
    for g in range(SWA_GROUP):
        sl = slice(g * TOKEN_BLOCK, (g + 1) * TOKEN_BLOCK)
        o_ref[0, g] = jnp.where(row < HEAD_DIM, o0[:, sl], o1[:, sl]).T.astype(BF16)


def _swa_call(layer, sink, qb, kb, vbt, band, n_lat):
    b, nslab, t, _ = qb.shape
    nt = t // TOKEN_BLOCK
    npair = kb.shape[1]
    assert 2 * WINDOW == TOKEN_BLOCK
    q_spec = pl.BlockSpec((1, SWA_GROUP, TOKEN_BLOCK, LANES), lambda bi, ji, ti, s: (bi, ji, ti, 0))
    k_spec = pl.BlockSpec((1, 1, t, LANES), lambda bi, ji, ti, s: (bi, ji, 0, 0))
    v_spec = pl.BlockSpec((1, 1, nt, VT_ROWS, TOKEN_BLOCK), lambda bi, ji, ti, s: (bi, ji, 0, 0, 0))
    return pl.pallas_call(
        functools.partial(_swa_kernel, nlb=n_lat // TOKEN_BLOCK),
        grid_spec=pltpu.PrefetchScalarGridSpec(
            num_scalar_prefetch=1, grid=(b, npair, nt),
            in_specs=[q_spec, k_spec, v_spec, pl.BlockSpec(band.shape, lambda bi, ji, ti, s: (0, 0))],
            out_specs=q_spec),
        out_shape=jax.ShapeDtypeStruct(qb.shape, BF16),
        compiler_params=_cparams(3),
        name="swa_attn",
    )(sink, qb, kb, vbt, band)


def _merge_kernel(x_ref, ya_ref, yb_ref, gate_ref, mod_ref, wa_ref, wb_ref, wo_ref, g2_ref, wrt_ref,
                  xo_ref, h2_ref, afft_ref, *, d):
    nh = ya_ref.shape[1]
    ya = jnp.concatenate([ya_ref[0, h] for h in range(nh)], axis=1)
    yb = jnp.concatenate([yb_ref[0, h] for h in range(nh)], axis=1)
    za = jnp.dot(ya, wa_ref[0], preferred_element_type=F32)
    zb = jnp.dot(yb, wb_ref[0], preferred_element_type=F32)
    gate = gate_ref[0].astype(F32)
    u = gate[:, :d] * za + gate[:, d:] * zb
    z = jnp.dot(u.astype(BF16), wo_ref[0], preferred_element_type=F32)
    mod = mod_ref[0, 0]
    xn = x_ref[0] + mod[2:3] * z
    xo_ref[0] = xn
    ms = jnp.mean(xn * xn, axis=-1, keepdims=True)
    h2 = xn * lax.rsqrt(ms + EPS) * g2_ref[0]
    h2 = (h2 * (1.0 + mod[4:5]) + mod[3:4]).astype(BF16)
    h2_ref[0] = h2
    lgt = lax.dot_general(wrt_ref[0], h2, (((1,), (1,)), ((), ())), preferred_element_type=F32)
    et = jnp.exp(lgt - jnp.max(lgt, axis=0, keepdims=True))
    afft_ref[0, 0] = et / jnp.sum(et, axis=0, keepdims=True)


def _merge_call(layer, xs, ya, yb, gates, modt, wa, wb, wo, norm2_g, wrt, n_lat):
    b, t, d = xs.shape
    nt = t // TOKEN_BLOCK
    nlb = n_lat // TOKEN_BLOCK
    nh = ya.shape[1]
    ne = wrt.shape[1]
    head_spec = pl.BlockSpec((1, nh, TOKEN_BLOCK, LANES), lambda bi, ti: (bi, 0, ti, 0))
    w_spec = pl.BlockSpec((1, d, d), lambda bi, ti: (layer, 0, 0))
    tok_spec = pl.BlockSpec((1, TOKEN_BLOCK, d), lambda bi, ti: (bi, ti, 0))
    return pl.pallas_call(
        functools.partial(_merge_kernel, d=d),
        grid=(b, nt),
        in_specs=[tok_spec, head_spec, head_spec,
                  pl.BlockSpec((1, TOKEN_BLOCK, 2 * d), lambda bi, ti: (bi, ti, 0)),
                  pl.BlockSpec((1, 1, 6, d), lambda bi, ti: (bi, ti // nlb, 0, 0)),
                  w_spec, w_spec, w_spec,
                  pl.BlockSpec((1, 1, d), lambda bi, ti: (layer, 0, 0)),
                  pl.BlockSpec((1, ne, d), lambda bi, ti: (layer, 0, 0))],
        out_specs=[tok_spec, tok_spec,
                   pl.BlockSpec((1, 1, ne, TOKEN_BLOCK), lambda bi, ti: (bi, ti, 0, 0))],
        out_shape=[jax.ShapeDtypeStruct((b, t, d), F32), jax.ShapeDtypeStruct((b, t, d), BF16),
                   jax.ShapeDtypeStruct((b, nt, ne, TOKEN_BLOCK), F32)],
        compiler_params=_cparams(2),
        name="merge",
    )(xs, ya, yb, gates, modt, wa, wb, wo, norm2_g, wrt)


def _route_kernel(afft_ref, tri_ref, pos_ref, cnt_ref, *, nlb, cap_l, cap_c):
    nt = afft_ref.shape[1]
    ne = afft_ref.shape[2]
    tri = tri_ref[...]

    def bits(blk):
        return lax.bitcast_convert_type(afft_ref[0, blk], I32)

    def count(pred_fn, blocks):
        acc = jnp.zeros((ne, TOKEN_BLOCK), F32)
        for blk in blocks:
            acc = acc + jnp.where(pred_fn(bits(blk)), 1.0, 0.0)
        return jnp.sum(acc, axis=1, keepdims=True)

    def select(blocks, cap, base, with_counts):
        def it(k, thr):
            cand = thr | jnp.left_shift(jnp.int32(1), 30 - k)
            return jnp.where(count(lambda bt: bt >= cand, blocks) >= cap, cand, thr)
        thr = lax.fori_loop(0, 31, it, jnp.zeros((ne, 1), I32))
        ties_wanted = cap - count(lambda bt: bt > thr, blocks)
        seen_eq = jnp.zeros((ne, 1), F32)
        seen_sel = jnp.zeros((ne, 1), F32)
        lane = lax.broadcasted_iota(I32, (ne, LANES), 1)
        cntv = jnp.zeros((ne, LANES), F32)
        for n, blk in enumerate(blocks):
            bt = bits(blk)
            eq = bt == thr
            eqf = jnp.where(eq, 1.0, 0.0)
            rank = jnp.dot(eqf.astype(BF16), tri, preferred_element_type=F32) + seen_eq
            self_ = jnp.where(eq, jnp.where(rank < ties_wanted, 1.0, 0.0), jnp.where(bt > thr, 1.0, 0.0))
            sel = self_ > 0.5
            slot = jnp.dot(self_.astype(BF16), tri, preferred_element_type=F32) + seen_sel
            pos_ref[0, blk] = jnp.where(sel, slot.astype(I32) + base, -1)
            if with_counts:
                cntv = jnp.where(lane == n, seen_sel, cntv)
            seen_eq = seen_eq + jnp.sum(eqf, axis=1, keepdims=True)
            seen_sel = seen_sel + jnp.sum(self_, axis=1, keepdims=True)
        if with_counts:
            cntv = jnp.where(lane == len(blocks), seen_sel, cntv)
            cnt_ref[0] = cntv.astype(I32)

    select(list(range(nlb)), cap_l, 0, True)
    select(list(range(nlb, nt)), cap_c, cap_l, False)


def _route_call(afft, tri, n_lat, cap_l, cap_c):
    b, nt, ne, _ = afft.shape
    return pl.pallas_call(
        functools.partial(_route_kernel, nlb=n_lat // TOKEN_BLOCK, cap_l=cap_l, cap_c=cap_c),
        grid=(b,),
        in_specs=[pl.BlockSpec((1, nt, ne, TOKEN_BLOCK), lambda bi: (bi, 0, 0, 0)),
                  pl.BlockSpec((TOKEN_BLOCK, TOKEN_BLOCK), lambda bi: (0, 0))],
        out_specs=[pl.BlockSpec((1, nt, ne, TOKEN_BLOCK), lambda bi: (bi, 0, 0, 0)),
                   pl.BlockSpec((1, ne, LANES), lambda bi: (bi, 0, 0))],
        out_shape=[jax.ShapeDtypeStruct(afft.shape, I32), jax.ShapeDtypeStruct((b, ne, LANES), I32)],
        compiler_params=_cparams(1),
        name="route",
    )(afft, tri)


def _slot_onehot(pos_row, first_slot, n_slots):
    slots = first_slot + lax.broadcasted_iota(I32, (n_slots, pos_row.shape[1]), 0)
    return jnp.where(pos_row == slots, 1.0, 0.0).astype(BF16)


def _gather_kernel(cnt_ref, h2_ref, pos_ref, xe_ref, *, nlb, cap_l, cap_c):
    bi = pl.program_id(0)
    e = pl.program_id(1)
    ne = pl.num_programs(1)
    cbase = (bi * ne + e) * (nlb + 1)
    win = GATHER_WINDOW
    xe_ref[0, 0, :cap_l, :] = jnp.zeros((cap_l, xe_ref.shape[3]), BF16)

    def slot_range(blk):
        return cnt_ref[cbase + blk], cnt_ref[cbase + blk + 1]

    def window_start(lo):
        return jnp.minimum((lo // BF16_ROWS) * BF16_ROWS, cap_l - win)

    def add_rows(blk, t0, s0, n):
        onehot = _slot_onehot(pos_ref[0, blk, pl.ds(e, 1), :], s0, n)
        xe_ref[0, 0, pl.ds(s0, n), :] += jnp.dot(onehot, h2_ref[0, pl.ds(t0, TOKEN_BLOCK), :],
                                                 preferred_element_type=F32).astype(BF16)

    fits = jnp.bool_(True)
    for blk in range(nlb):
        lo, hi = slot_range(blk)
        fits = fits & (hi <= window_start(lo) + win)

    @pl.when(fits)
    def _():
        for blk in range(nlb):
            s0 = pl.multiple_of(window_start(slot_range(blk)[0]), BF16_ROWS)
            add_rows(blk, blk * TOKEN_BLOCK, s0, win)

    @pl.when(jnp.logical_not(fits))
    def _():
        def body(blk, _):
            lo, hi = slot_range(blk)

            def chunk(c, _):
                add_rows(blk, pl.multiple_of(blk * TOKEN_BLOCK, TOKEN_BLOCK),
                         pl.multiple_of(c * SLOT_CHUNK, SLOT_CHUNK), SLOT_CHUNK)
                return 0

            lax.fori_loop(lo // SLOT_CHUNK, jnp.where(hi > lo, (hi + SLOT_CHUNK - 1) // SLOT_CHUNK, 0), chunk, 0)
            return 0

        lax.fori_loop(0, nlb, body, 0)

    ctx_acc = jnp.zeros((cap_c, h2_ref.shape[2]), F32)
    for blk in range(nlb, pos_ref.shape[1]):
        onehot = _slot_onehot(pos_ref[0, blk, pl.ds(e, 1), :], cap_l, cap_c)
        t0 = blk * TOKEN_BLOCK
        ctx_acc = ctx_acc + jnp.dot(onehot, h2_ref[0, t0:t0 + TOKEN_BLOCK, :], preferred_element_type=F32)
    xe_ref[0, 0, cap_l:cap_l + cap_c, :] = ctx_acc.astype(BF16)


def _gather_call(cnt_flat, h2, pos, n_lat, cap_l, cap_c):
    b, t, d = h2.shape
    _, nt, ne, _ = pos.shape
    slots = cap_l + cap_c
    return pl.pallas_call(
        functools.partial(_gather_kernel, nlb=n_lat // TOKEN_BLOCK, cap_l=cap_l, cap_c=cap_c),
        grid_spec=pltpu.PrefetchScalarGridSpec(
            num_scalar_prefetch=1, grid=(b, ne),
            in_specs=[pl.BlockSpec((1, t, d), lambda bi, ei, s: (bi, 0, 0)),
                      pl.BlockSpec((1, nt, ne, TOKEN_BLOCK), lambda bi, ei, s: (bi, 0, 0, 0))],
            out_specs=pl.BlockSpec((1, 1, slots, d), lambda bi, ei, s: (bi, ei, 0, 0))),
        out_shape=jax.ShapeDtypeStruct((b, ne, slots, d), BF16),
        compiler_params=_cparams(2),
        name="moe_gather",
    )(cnt_flat, h2, pos)


def _ffn_kernel(x_ref, w1_ref, w3_ref, w2_ref, o_ref):
    x = x_ref[0, 0]
    ff = w1_ref.shape[-1]
    acc = None
    for f0 in range(0, ff, FF_CHUNK):
        f1 = min(ff, f0 + FF_CHUNK)
        a = jnp.dot(x, w1_ref[0, 0, :, f0:f1], preferred_element_type=F32)
        g = jnp.dot(x, w3_ref[0, 0, :, f0:f1], preferred_element_type=F32)
        hid = (a * jax.nn.sigmoid(a) * g).astype(BF16)
        part = jnp.dot(hid, w2_ref[0, 0, f0:f1, :], preferred_element_type=F32)
        acc = part if acc is None else acc + part
    o_ref[0, 0] = acc.astype(BF16)


def _ffn_call(layer, xe, w1, w3, w2):
    b, ne, slots, d = xe.shape
    ff = w1.shape[-1]
    x_spec = pl.BlockSpec((1, 1, slots, d), lambda ei, bi: (bi, ei, 0, 0))
    return pl.pallas_call(
        _ffn_kernel,
        grid=(ne, b),
        in_specs=[x_spec,
                  pl.BlockSpec((1, 1, d, ff), lambda ei, bi: (layer, ei, 0, 0)),
                  pl.BlockSpec((1, 1, d, ff), lambda ei, bi: (layer, ei, 0, 0)),
                  pl.BlockSpec((1, 1, ff, d), lambda ei, bi: (layer, ei, 0, 0))],
        out_specs=x_spec,
        out_shape=jax.ShapeDtypeStruct(xe.shape, BF16),
        compiler_params=_cparams(2),
        name="moe_ffn",
    )(xe, w1, w3, w2)


def _combine_kernel(cnt_ref, x_ref, ye_ref, pos_ref, afft_ref, mod_ref, xo_ref, acc_ref, *, nlb, cap_l, cap_c):
    bi = pl.program_id(0)
    ti = pl.program_id(1)
    ne = ye_ref.shape[1]
    is_lat = ti < nlb
    tn = (((0,), (0,)), ((), ()))
    win = min(2 * SLOT_CHUNK, cap_l)

    def slot_range(e):
        cbase = (bi * ne + e) * (nlb + 1) + jnp.minimum(ti, nlb - 1)
        return cnt_ref[cbase], cnt_ref[cbase + 1]

    def window_start(lo):
        return jnp.minimum((lo // SLOT_CHUNK) * SLOT_CHUNK, cap_l - win)

    def gated_onehot(e, first_slot, n_slots):
        slots = first_slot + lax.broadcasted_iota(I32, (n_slots, TOKEN_BLOCK), 0)
        return jnp.where(pos_ref[0, 0, e:e + 1, :] == slots, afft_ref[0, 0, e:e + 1, :], 0.0).astype(BF16)

    def residual(y):
        xo_ref[0] = x_ref[0] + mod_ref[0, 0][5:6] * y

    fits = is_lat
    for e in range(ne):
        lo, hi = slot_range(e)
        fits = fits & (hi <= window_start(lo) + win)

    @pl.when(fits)
    def _():
        acc = None
        for e in range(ne):
            s0 = pl.multiple_of(window_start(slot_range(e)[0]), SLOT_CHUNK)
            part = lax.dot_general(gated_onehot(e, s0, win), ye_ref[0, e, pl.ds(s0, win), :], tn,
                                   preferred_element_type=F32)
            acc = part if acc is None else acc + part
        residual(acc)

    @pl.when(is_lat & jnp.logical_not(fits))
    def _():
        acc_ref[...] = jnp.zeros_like(acc_ref)
        for e in range(ne):
            lo, hi = slot_range(e)

            def body(c, _, e=e):
                s0 = pl.multiple_of(c * SLOT_CHUNK, SLOT_CHUNK)
                acc_ref[...] += lax.dot_general(gated_onehot(e, s0, SLOT_CHUNK),
                                                ye_ref[0, e, pl.ds(s0, SLOT_CHUNK), :], tn,
                                                preferred_element_type=F32)
                return 0

            lax.fori_loop(lo // SLOT_CHUNK, jnp.where(hi > lo, (hi + SLOT_CHUNK - 1) // SLOT_CHUNK, 0), body, 0)
        residual(acc_ref[...])

    @pl.when(jnp.logical_not(is_lat))
    def _():
        acc = None
        for e in range(ne):
            part = lax.dot_general(gated_onehot(e, cap_l, cap_c), ye_ref[0, e, cap_l:cap_l + cap_c, :], tn,
                                   preferred_element_type=F32)
            acc = part if acc is None else acc + part
        residual(acc)


def _combine_call(cnt_flat, xs, ye, pos, afft, modt, n_lat, cap_l, cap_c):
    b, t, d = xs.shape
    _, nt, ne, _ = pos.shape
    nlb = n_lat // TOKEN_BLOCK
    slots = cap_l + cap_c
    tok_spec = pl.BlockSpec((1, TOKEN_BLOCK, d), lambda bi, ti, s: (bi, ti, 0))
    return pl.pallas_call(
        functools.partial(_combine_kernel, nlb=nlb, cap_l=cap_l, cap_c=cap_c),
        grid_spec=pltpu.PrefetchScalarGridSpec(
            num_scalar_prefetch=1, grid=(b, nt),
            in_specs=[tok_spec,
                      pl.BlockSpec((1, ne, slots, d), lambda bi, ti, s: (bi, 0, 0, 0)),
                      pl.BlockSpec((1, 1, ne, TOKEN_BLOCK), lambda bi, ti, s: (bi, ti, 0, 0)),
                      pl.BlockSpec((1, 1, ne, TOKEN_BLOCK), lambda bi, ti, s: (bi, ti, 0, 0)),
                      pl.BlockSpec((1, 1, 6, d), lambda bi, ti, s: (bi, ti // nlb, 0, 0))],
            out_specs=tok_spec,
            scratch_shapes=[pltpu.VMEM((TOKEN_BLOCK, d), F32)]),
        out_shape=jax.ShapeDtypeStruct(xs.shape, F32),
        compiler_params=_cparams(2),
        name="moe_combine",
    )(cnt_flat, xs, ye, pos, afft, modt)


def _rope_tables(n_lat, n_ctx):
    rows = n_lat // GRID_W
    row = jnp.repeat(jnp.arange(rows), GRID_W).astype(F32)
    col = jnp.tile(jnp.arange(GRID_W), rows).astype(F32)
    half = HEAD_DIM // 2
    inv_freq = ROPE_THETA ** (-jnp.arange(0, half, 2, dtype=F32) / half)
    ar = row[:, None] * inv_freq
    ac = col[:, None] * inv_freq
    cos64 = jnp.concatenate([jnp.cos(ar), jnp.cos(ar), jnp.cos(ac), jnp.cos(ac)], axis=1)
    sin64 = jnp.concatenate([-jnp.sin(ar), jnp.sin(ar), -jnp.sin(ac), jnp.sin(ac)], axis=1)
    cos_t = jnp.concatenate([jnp.tile(cos64, (1, LANES // HEAD_DIM)), jnp.ones((n_ctx, LANES), F32)], axis=0)
    sin_t = jnp.concatenate([jnp.tile(sin64, (1, LANES // HEAD_DIM)), jnp.zeros((n_ctx, LANES), F32)], axis=0)
    return cos_t, sin_t


def _swa_head_order():
    heads = []
    for j in range(SWA_KV_HEADS // 2):
        for g in range(SWA_GROUP):
            heads += [2 * SWA_GROUP * j + g, 2 * SWA_GROUP * j + SWA_GROUP + g]
    return np.concatenate([np.arange(h * HEAD_DIM, (h + 1) * HEAD_DIM) for h in heads])


def kernel(x, c, ctx, c_ctx, w_ada, b_ada, norm1_g, w_in, b_gate, diff_q_g, diff_k_g, diff_lambda, diff_subln_g,
           swa_q_g, swa_k_g, swa_sink, w_branch_a, w_branch_b, w_out, norm2_g, w_router, w_e1, w_e3, w_e2):
    b, n_lat, d = x.shape
    n_ctx = ctx.shape[1]
    depth = w_ada.shape[0]
    assert n_lat % TOKEN_BLOCK == 0 and n_ctx % TOKEN_BLOCK == 0 and n_lat >= 2 * TOKEN_BLOCK
    assert d == DIFF_HEADS * 2 * HEAD_DIM == SWA_Q_HEADS * HEAD_DIM and w_router.shape[-1] == N_EXPERTS
    cap_l = CAPACITY_FACTOR * n_lat // N_EXPERTS
    cap_c = CAPACITY_FACTOR * n_ctx // N_EXPERTS
    assert cap_l % SLOT_CHUNK == 0 and cap_c % 16 == 0
    nlb = n_lat // TOKEN_BLOCK

    perm = _swa_head_order()
    kvw = SWA_KV_HEADS * HEAD_DIM
    o = np.cumsum([0, d, d, d, d, kvw, kvw, d, d])
    w_in_p = jnp.concatenate(
        [w_in[..., o[0]:o[3]], w_in[..., o[3]:o[4]][..., perm], w_in[..., o[6]:o[8]], w_in[..., o[4]:o[6]]],
        axis=-1).astype(BF16)
    wa = w_branch_a.astype(BF16)
    wb = w_branch_b[:, perm, :].astype(BF16)
    wo = w_out.astype(BF16)
    wrt =jnp.swapaxes(w_router, 1, 2).astype(BF16)
    w1 = w_e1.astype(BF16)
    w3 = w_e3.astype(BF16)
    w2 = w_e2.astype(BF16)
    scale = HEAD_DIM ** -0.5
    hg = jnp.stack([jnp.tile(diff_q_g, (1, d // HEAD_DIM)) * (scale * math.log2(math.e)),
                    jnp.tile(diff_k_g, (1, d // HEAD_DIM)),
                    jnp.tile(swa_q_g, (1, d // HEAD_DIM)) * (scale * math.log2(math.e)),
                    jnp.tile(swa_k_g, (1, d // HEAD_DIM))], axis=1)
    subln = diff_subln_g.reshape(depth, 1, LANES)
    g1n = norm1_g.reshape(depth, 1, d)
    g2n = norm2_g.reshape(depth, 1, d)
    bg = b_gate.reshape(depth, 1, 2 * d)
    gm = jnp.asarray(np.kron(np.eye(TOKEN_BLOCK // HEAD_DIM), np.ones((HEAD_DIM, HEAD_DIM))), BF16)
    tri = jnp.asarray(np.triu(np.ones((TOKEN_BLOCK, TOKEN_BLOCK)), 1), BF16)
    cos_t, sin_t = _rope_tables(n_lat, n_ctx)
    key_i = np.arange(2 * TOKEN_BLOCK)[:, None]
    qry_i = np.arange(TOKEN_BLOCK)[None, :]
    band = np.where(np.abs(qry_i + WINDOW - key_i) <= WINDOW, 0.0, NEG_INF)
    band = jnp.asarray(np.concatenate([band, np.zeros((n_ctx, TOKEN_BLOCK))], axis=0), BF16)

    rows = -(-(b + 1) // 8) * 8
    cc = jnp.concatenate([c, c_ctx[None], jnp.zeros((rows - b - 1, d), F32)], axis=0)
    mod_all = _ada_call(cc, w_ada, b_ada)

    xs = jnp.concatenate([x, ctx], axis=1)
    for i in range(depth):
        lam_init = 0.8 - 0.6 * math.exp(-0.3 * i)
        mod_l = mod_all[i, :b].reshape(b, 1, 6, d)
        mod_c = jnp.broadcast_to(mod_all[i, b].reshape(1, 1, 6, d), (b, 1, 6, d))
        modt = jnp.concatenate([mod_l, mod_c], axis=1)
        qa, ka, va, qb, gates, kb, vb = _inproj_call(i, xs, modt, g1n, w_in_p, gm, hg, bg, cos_t, sin_t, n_lat)
        ya = _diff_attn_call(i, qa, ka, va, diff_lambda, subln, n_lat, lam_init)
        yb = _swa_call(i, swa_sink[i], qb, kb, vb, band, n_lat)
        xs, h2, afft = _merge_call(i, xs, ya, yb, gates, modt, wa, wb, wo, g2n, wrt, n_lat)
        pos, cnt = _route_call(afft, tri, n_lat, cap_l, cap_c)
        cnt_flat = cnt[:, :, :nlb + 1].reshape(-1)
        xe = _gather_call(cnt_flat, h2, pos, n_lat, cap_l, cap_c)
        ye = _ffn_call(i, xe, w1, w3, w2)
        xs = _combine_call(cnt_flat, xs, ye, pos, afft, modt, n_lat, cap_l, cap_c)
    return xs[:, :n_lat]
```

```python
import functools
import math

import numpy as np
import jax
import jax.numpy as jnp
from jax import lax
from jax.experimental import pallas as pl
from jax.experimental.pallas import tpu as pltpu

F32 = jnp.float32
BF16 = jnp.bfloat16
I32 = jnp.int32

HEAD_DIM = 64
GRID_W = 64
DIFF_HEADS = 8
SWA_Q_HEADS = 16
SWA_KV_HEADS = 4
SWA_GROUP = SWA_Q_HEADS // SWA_KV_HEADS
WINDOW = 128
N_EXPERTS = 16
CAPACITY_FACTOR = 2
ROPE_THETA = 10000.0
EPS = 1e-6
NEG_INF = -1e30

LANES = 128
TOKEN_BLOCK = 256
SLOT_CHUNK = 128
BF16_ROWS = 16
SCORE_BOUND_LOG2 = 48.0
GATHER_WINDOW = 64
PROJ_CHUNK = 512
FF_CHUNK = 1536
VT_ROWS = LANES + 16
VMEM_LIMIT = 56 * 1024 * 1024


def _cparams(n_axes):
    return pltpu.CompilerParams(dimension_semantics=("arbitrary",) * n_axes, vmem_limit_bytes=VMEM_LIMIT)


def _ada_kernel(c_ref, w_ref, b_ref, o_ref):
    c = c_ref[...]
    sc = c * jax.nn.sigmoid(c)
    o_ref[0] = jnp.dot(sc, w_ref[0], preferred_element_type=F32) + b_ref[0]


def _ada_call(cc, w_ada, b_ada):
    depth, d, six_d = w_ada.shape
    rows = cc.shape[0]
    cols = 1536
    return pl.pallas_call(
        _ada_kernel,
        grid=(depth, six_d // cols),
        in_specs=[pl.BlockSpec((rows, d), lambda i, j: (0, 0)),
                  pl.BlockSpec((1, d, cols), lambda i, j: (i, 0, j)),
                  pl.BlockSpec((1, 1, cols), lambda i, j: (i, 0, j))],
        out_specs=pl.BlockSpec((1, rows, cols), lambda i, j: (i, 0, j)),
        out_shape=jax.ShapeDtypeStruct((depth, rows, six_d), F32),
        compiler_params=_cparams(2),
        name="ada",
    )(cc, w_ada, b_ada.reshape(depth, 1, six_d))


def _inproj_kernel(x_ref, mod_ref, g1_ref, w_ref, gm_ref, hg_ref, bg_ref, cos_ref, sin_ref,
                   qa_ref, ka_ref, va_ref, qb_ref, gate_ref, kb_ref, vb_ref, *, d):
    x = x_ref[0]
    mod = mod_ref[0, 0]
    ms = jnp.mean(x * x, axis=-1, keepdims=True)
    h = x * lax.rsqrt(ms + EPS) * g1_ref[0]
    h = (h * (1.0 + mod[1:2]) + mod[0:1]).astype(BF16)
    cos = cos_ref[...]
    sin = sin_ref[...]
    lane = lax.broadcasted_iota(I32, (1, LANES), 1)
    first_half = (lane % 32) < 16
    gm = gm_ref[...]

    def proj(c0, width):
        return jnp.dot(h, w_ref[0, :, c0:c0 + width], preferred_element_type=F32)

    def head_norm_rope(p, gain):
        w = p.shape[1]
        gw = gm.shape[0]
        sq = (p * p).astype(BF16)
        msq = jnp.concatenate([jnp.dot(sq[:, g0:g0 + gw], gm, preferred_element_type=F32)
                               for g0 in range(0, w, gw)], axis=1) * (1.0 / HEAD_DIM)
        qn = p * lax.rsqrt(msq + EPS) * gain
        outs = []
        for u in range(w // LANES):
            seg = qn[:, u * LANES:(u + 1) * LANES]
            partner = jnp.where(first_half, pltpu.roll(seg, LANES - 16, 1), pltpu.roll(seg, 16, 1))
            outs.append((seg * cos + partner * sin).astype(BF16))
        return outs

    per = PROJ_CHUNK // LANES
    for sec, (out_ref, gain_row) in enumerate(((qa_ref, 0), (ka_ref, 1))):
        for c in range(d // PROJ_CHUNK):
            c0 = sec * d + c * PROJ_CHUNK
            outs = head_norm_rope(proj(c0, PROJ_CHUNK), hg_ref[0, gain_row:gain_row + 1, c * PROJ_CHUNK:(c + 1) * PROJ_CHUNK])
            for u, o in enumerate(outs):
                out_ref[0, c * per + u] = o
    for c in range(d // PROJ_CHUNK):
        p = proj(2 * d + c * PROJ_CHUNK, PROJ_CHUNK)
        for u in range(per):
            va_ref[0, c * per + u, 0, :LANES, :] = p[:, u * LANES:(u + 1) * LANES].T.astype(BF16)
            va_ref[0, c * per + u, 0, LANES:, :] = jnp.ones((VT_ROWS - LANES, TOKEN_BLOCK), BF16)
    for c in range(d // PROJ_CHUNK):
        outs = head_norm_rope(proj(3 * d + c * PROJ_CHUNK, PROJ_CHUNK), hg_ref[0, 2:3, c * PROJ_CHUNK:(c + 1) * PROJ_CHUNK])
        for u, o in enumerate(outs):
            qb_ref[0, c * per + u] = o
    for c in range(2 * d // PROJ_CHUNK):
        p = proj(4 * d + c * PROJ_CHUNK, PROJ_CHUNK) + bg_ref[0, :, c * PROJ_CHUNK:(c + 1) * PROJ_CHUNK]
        gate_ref[0, :, c * PROJ_CHUNK:(c + 1) * PROJ_CHUNK] = jax.nn.sigmoid(p).astype(BF16)
    kvw = SWA_KV_HEADS * HEAD_DIM
    outs = head_norm_rope(proj(6 * d, kvw), hg_ref[0, 3:4, :kvw])
    for u, o in enumerate(outs):
        kb_ref[0, u] = o
    p = proj(6 * d + kvw, kvw)
    for u in range(kvw // LANES):
        vb_ref[0, u, 0, :LANES, :] = p[:, u * LANES:(u + 1) * LANES].T.astype(BF16)
        vb_ref[0, u, 0, LANES:, :] = jnp.ones((VT_ROWS - LANES, TOKEN_BLOCK), BF16)


def _inproj_call(layer, xs, modt, norm1_g, w_in_p, gm, hg, bg, cos_t, sin_t, n_lat):
    b, t, d = xs.shape
    nt = t // TOKEN_BLOCK
    nlb = n_lat // TOKEN_BLOCK
    in_w = w_in_p.shape[-1]
    nh = d // LANES
    nkv = SWA_KV_HEADS * HEAD_DIM // LANES
    head_shape = jax.ShapeDtypeStruct((b, nh, t, LANES), BF16)
    kv_shape = jax.ShapeDtypeStruct((b, nkv, t, LANES), BF16)
    head_spec = pl.BlockSpec((1, nh, TOKEN_BLOCK, LANES), lambda bi, ti: (bi, 0, ti, 0))
    kv_spec = pl.BlockSpec((1, nkv, TOKEN_BLOCK, LANES), lambda bi, ti: (bi, 0, ti, 0))
    vt_shape = jax.ShapeDtypeStruct((b, nh, nt, VT_ROWS, TOKEN_BLOCK), BF16)
    vt_spec = pl.BlockSpec((1, nh, 1, VT_ROWS, TOKEN_BLOCK), lambda bi, ti: (bi, 0, ti, 0, 0))
    kvt_shape = jax.ShapeDtypeStruct((b, nkv, nt, VT_ROWS, TOKEN_BLOCK), BF16)
    kvt_spec = pl.BlockSpec((1, nkv, 1, VT_ROWS, TOKEN_BLOCK), lambda bi, ti: (bi, 0, ti, 0, 0))
    return pl.pallas_call(
        functools.partial(_inproj_kernel, d=d),
        grid=(b, nt),
        in_specs=[pl.BlockSpec((1, TOKEN_BLOCK, d), lambda bi, ti: (bi, ti, 0)),
                  pl.BlockSpec((1, 1, 6, d), lambda bi, ti: (bi, ti // nlb, 0, 0)),
                  pl.BlockSpec((1, 1, d), lambda bi, ti: (layer, 0, 0)),
                  pl.BlockSpec((1, d, in_w), lambda bi, ti: (layer, 0, 0)),
                  pl.BlockSpec(gm.shape, lambda bi, ti: (0, 0)),
                  pl.BlockSpec((1, 4, d), lambda bi, ti: (layer, 0, 0)),
                  pl.BlockSpec((1, 1, 2 * d), lambda bi, ti: (layer, 0, 0)),
                  pl.BlockSpec((TOKEN_BLOCK, LANES), lambda bi, ti: (ti, 0)),
                  pl.BlockSpec((TOKEN_BLOCK, LANES), lambda bi, ti: (ti, 0))],
        out_specs=[head_spec, head_spec, vt_spec, head_spec,
                   pl.BlockSpec((1, TOKEN_BLOCK, 2 * d), lambda bi, ti: (bi, ti, 0)),
                   kv_spec, kvt_spec],
        out_shape=[head_shape, head_shape, vt_shape, head_shape,
                   jax.ShapeDtypeStruct((b, t, 2 * d), BF16), kv_shape, kvt_shape],
        compiler_params=_cparams(2),
        name="inproj",
    )(xs, modt, norm1_g, w_in_p, gm, hg, bg, cos_t, sin_t)


def _diff_attn_kernel(flag_ref, q_ref, k_ref, v_ref, dl_ref, sg_ref, *rest, key_blocks, bpu, lam_init):
    o_ref = rest[-1]
    q = q_ref[0, 0]
    lane = lax.broadcasted_iota(I32, q.shape, 1)
    zero = jnp.zeros_like(q)
    q2 = jnp.concatenate([jnp.where(lane < HEAD_DIM, q, zero), jnp.where(lane >= HEAD_DIM, q, zero)], axis=0)
    nq = q2.shape[0]
    nn_t = (((1,), (1,)), ((), ()))

    def scores(blk):
        k = k_ref[0, 0, blk * TOKEN_BLOCK:(blk + 1) * TOKEN_BLOCK, :]
        return lax.dot_general(k, q2, nn_t, preferred_element_type=F32).astype(BF16)

    def process(blocks, s_blocks, state, next_blocks):
        m, l, acc = state
        m_blk = s_blocks[0]
        for s in s_blocks[1:]:
            m_blk = jnp.maximum(m_blk, s)
        m_new = jnp.maximum(m, jnp.max(m_blk, axis=0, keepdims=True).astype(F32))
        alpha = jnp.exp2(m - m_new)
        m_bf = m_new.astype(BF16)
        s_next = []
        r = None
        for c, (blk, s) in enumerate(zip(blocks, s_blocks)):
            if c < len(next_blocks):
                s_next.append(scores(next_blocks[c]))
            p = jnp.exp2(s - m_bf)
            part = jnp.dot(v_ref[0, 0, blk], p, preferred_element_type=F32)
            r = part if r is None else r + part
        s_next += [scores(blk) for blk in next_blocks[len(blocks):]]
        return (m_new, alpha * l + r[LANES:LANES + 1], alpha * acc + r[:LANES]), s_next

    init = (jnp.full((1, nq), NEG_INF, F32), jnp.zeros((1, nq), F32), jnp.zeros((LANES, nq), F32))

    def finish(state):
        _, l, acc = state
        o = acc / l
        dl = dl_ref[0]
        lam = (jnp.exp(jnp.sum(dl[0:1] * dl[1:2], axis=1, keepdims=True))
               - jnp.exp(jnp.sum(dl[2:3] * dl[3:4], axis=1, keepdims=True)) + lam_init)
        y = (o[:, :nq // 2] - lam * o[:, nq // 2:]).T
        y = y * lax.rsqrt(jnp.mean(y * y, axis=-1, keepdims=True) + EPS) * sg_ref[0] * (1.0 - lam_init)
        o_ref[0, 0] = y.astype(BF16)

    def bounded(blocks):
        r = None
        s = scores(blocks[0])
        for n, blk in enumerate(blocks):
            s_next = scores(blocks[n + 1]) if n + 1 < len(blocks) else None
            part = jnp.dot(v_ref[0, 0, blk], jnp.exp2(s), preferred_element_type=F32)
            r = part if r is None else r + part
            s = s_next
        finish((None, r[LANES:LANES + 1], r[:LANES]))

    @pl.when(flag_ref[0] > 0)
    def _():
        bounded(key_blocks)

    @pl.when(flag_ref[0] <= 0)
    def _():
        units = [key_blocks[u:u + bpu] for u in range(0, len(key_blocks), bpu)]
        state = init
        s_cur = [scores(blk) for blk in units[0]]
        for u, blocks in enumerate(units):
            state, s_cur = process(blocks, s_cur, state, units[u + 1] if u + 1 < len(units) else [])
        finish(state)


def _diff_attn_call(layer, flag, qa, ka, vat, diff_lambda, subln_g, n_lat, lam_init):
    b, nh, t, _ = qa.shape
    nt = t // TOKEN_BLOCK
    nlb = n_lat // TOKEN_BLOCK
    n_ctx = t - n_lat
    assert n_lat % n_ctx == 0
    tq = next(u for u in (2 * TOKEN_BLOCK, TOKEN_BLOCK) if n_lat % u == 0)
    k_spec = pl.BlockSpec((1, 1, t, LANES), lambda bi, hi, ti, f: (bi, hi, 0, 0))
    v_spec = pl.BlockSpec((1, 1, nt, VT_ROWS, TOKEN_BLOCK), lambda bi, hi, ti, f: (bi, hi, 0, 0, 0))
    par_specs = [pl.BlockSpec((1, 4, HEAD_DIM), lambda bi, hi, ti, f: (layer, 0, 0)),
                 pl.BlockSpec((1, 1, LANES), lambda bi, hi, ti, f: (layer, 0, 0))]

    def call(rows, first_block, n_steps, key_blocks, prev):
        q_spec = pl.BlockSpec((1, 1, rows, LANES), lambda bi, hi, ti, f: (bi, hi, first_block + ti, 0))
        extra = [] if prev is None else [pl.BlockSpec(memory_space=pl.ANY)]
        operands = (flag, qa, ka, vat, diff_lambda, subln_g) + (() if prev is None else (prev,))
        return pl.pallas_call(
            functools.partial(_diff_attn_kernel, key_blocks=key_blocks, bpu=4, lam_init=lam_init),
            grid_spec=pltpu.PrefetchScalarGridSpec(
                num_scalar_prefetch=1, grid=(b, nh, n_steps),
                in_specs=[q_spec, k_spec, v_spec] + par_specs + extra, out_specs=q_spec),
            out_shape=jax.ShapeDtypeStruct(qa.shape, BF16),
            input_output_aliases={} if prev is None else {len(operands) - 1: 0},
            compiler_params=_cparams(3),
            name="diff_attn",
        )(*operands)

    ya = call(tq, 0, n_lat // tq, list(range(nt)), None)
    return call(n_ctx, n_lat // n_ctx, 1, list(range(nlb, nt)), ya)


def _swa_kernel(flag_ref, sink_ref, q_ref, k_ref, v_ref, band_ref, o_ref, *, nlb):
    j = pl.program_id(1)
    i = pl.program_id(2)
    nt = v_ref.shape[2]
    n_lat = nlb * TOKEN_BLOCK
    is_lat = i < nlb
    im1 = jnp.maximum(i - 1, 0)
    ip1 = jnp.minimum(i + 1, nt - 1)

    def krows(blk, lo, n):
        return k_ref[0, 0, pl.ds(pl.multiple_of(blk * TOKEN_BLOCK + lo, WINDOW), n), :]

    kcat = jnp.concatenate([krows(im1, WINDOW, WINDOW), krows(i, 0, TOKEN_BLOCK), krows(ip1, 0, WINDOW),
                            k_ref[0, 0, n_lat:, :]], axis=0)
    vcat = jnp.concatenate([v_ref[0, 0, im1][:, WINDOW:], v_ref[0, 0, i], v_ref[0, 0, ip1][:, :WINDOW]]
                           + [v_ref[0, 0, blk] for blk in range(nlb, nt)], axis=1)
    nk = kcat.shape[0]
    neg = jnp.float32(NEG_INF)
    zero = jnp.float32(0.0)
    pen = (jnp.where((i == 0) | jnp.logical_not(is_lat), neg, zero), jnp.where(is_lat, zero, neg),
           jnp.where(i >= nlb - 1, neg, zero))
    part = jnp.concatenate([jnp.full((WINDOW, TOKEN_BLOCK), pen[0], F32), jnp.full((TOKEN_BLOCK, TOKEN_BLOCK), pen[1], F32),
                            jnp.full((WINDOW, TOKEN_BLOCK), pen[2], F32),
                            jnp.zeros((nk - 2 * TOKEN_BLOCK, TOKEN_BLOCK), F32)], axis=0)
    bias1 = band_ref[...] + part.astype(BF16)
    bias = jnp.concatenate([bias1] * SWA_GROUP, axis=1)
    lane = lax.broadcasted_iota(I32, (TOKEN_BLOCK, LANES), 1)
    nn_t = (((1,), (1,)), ((), ()))
    log2e = math.log2(math.e)

    def scores(half):
        in_half = (lane >= half * HEAD_DIM) & (lane < (half + 1) * HEAD_DIM)
        qs = jnp.concatenate([jnp.where(in_half, q_ref[0, g], jnp.zeros((TOKEN_BLOCK, LANES), BF16))
                              for g in range(SWA_GROUP)], axis=0)
        return lax.dot_general(kcat, qs, nn_t, preferred_element_type=F32).astype(BF16) + bias

    def attend(s, half, bounded):
        sink = jnp.concatenate(
            [jnp.full((1, TOKEN_BLOCK), sink_ref[2 * SWA_GROUP * j + SWA_GROUP * half + g] * log2e, F32)
             for g in range(SWA_GROUP)], axis=1)
        if bounded:
            r = jnp.dot(vcat, jnp.exp2(s), preferred_element_type=F32)
            return r[:LANES] / (r[LANES:LANES + 1] + jnp.exp2(sink))
        m = jnp.maximum(jnp.max(s, axis=0, keepdims=True).astype(F32), sink).astype(BF16)
        p = jnp.exp2(s - m)
        r = jnp.dot(vcat, p, preferred_element_type=F32)
        return r[:LANES] / (r[LANES:LANES + 1] + jnp.exp2(sink - m.astype(F32)))

    def run(bounded):
        s0 = scores(0)
        s1 = scores(1)
        o0 = attend(s0, 0, bounded)
        o1 = attend(s1, 1, bounded)
        row = lax.broadcasted_iota(I32, (LANES, TOKEN_BLOCK), 0)
        for g in range(SWA_GROUP):
            sl = slice(g * TOKEN_BLOCK, (g + 1) * TOKEN_BLOCK)
            o_ref[0, g] = jnp.where(row < HEAD_DIM, o0[:, sl], o1[:, sl]).T.astype(BF16)

    @pl.when(flag_ref[0] > 0)
    def _():
        run(True)

    @pl.when(flag_ref[0] <= 0)
    def _():
        run(False)


def _swa_call(layer, flag, sink, qb, kb, vbt, band, n_lat):
    b, nslab, t, _ = qb.shape
    nt = t // TOKEN_BLOCK
    npair = kb.shape[1]
    assert 2 * WINDOW == TOKEN_BLOCK
    q_spec = pl.BlockSpec((1, SWA_GROUP, TOKEN_BLOCK, LANES), lambda bi, ji, ti, f, s: (bi, ji, ti, 0))
    k_spec = pl.BlockSpec((1, 1, t, LANES), lambda bi, ji, ti, f, s: (bi, ji, 0, 0))
    v_spec = pl.BlockSpec((1, 1, nt, VT_ROWS, TOKEN_BLOCK), lambda bi, ji, ti, f, s: (bi, ji, 0, 0, 0))
    return pl.pallas_call(
        functools.partial(_swa_kernel, nlb=n_lat // TOKEN_BLOCK),
        grid_spec=pltpu.PrefetchScalarGridSpec(
            num_scalar_prefetch=2, grid=(b, npair, nt),
            in_specs=[q_spec, k_spec, v_spec, pl.BlockSpec(band.shape, lambda bi, ji, ti, f, s: (0, 0))],
            out_specs=q_spec),
        out_shape=jax.ShapeDtypeStruct(qb.shape, BF16),
        compiler_params=_cparams(3),
        name="swa_attn",
    )(flag, sink, qb, kb, vbt, band)


def _merge_kernel(x_ref, ya_ref, yb_ref, gate_ref, mod_ref, wa_ref, wb_ref, wo_ref, g2_ref, wrt_ref,
                  xo_ref, h2_ref, afft_ref, *, d):
    nh = ya_ref.shape[1]
    ya = jnp.concatenate([ya_ref[0, h] for h in range(nh)], axis=1)
    yb = jnp.concatenate([yb_ref[0, h] for h in range(nh)], axis=1)
    za = jnp.dot(ya, wa_ref[0], preferred_element_type=F32)
    zb = jnp.dot(yb, wb_ref[0], preferred_element_type=F32)
    gate = gate_ref[0].astype(F32)
    u = gate[:, :d] * za + gate[:, d:] * zb
    z = jnp.dot(u.astype(BF16), wo_ref[0], preferred_element_type=F32)
    mod = mod_ref[0, 0]
    xn = x_ref[0] + mod[2:3] * z
    xo_ref[0] = xn
    ms = jnp.mean(xn * xn, axis=-1, keepdims=True)
    h2 = xn * lax.rsqrt(ms + EPS) * g2_ref[0]
    h2 = (h2 * (1.0 + mod[4:5]) + mod[3:4]).astype(BF16)
    h2_ref[0] = h2
    lgt = lax.dot_general(wrt_ref[0], h2, (((1,), (1,)), ((), ())), preferred_element_type=F32)
    et = jnp.exp(lgt - jnp.max(lgt, axis=0, keepdims=True))
    afft_ref[0, 0] = et / jnp.sum(et, axis=0, keepdims=True)


def _merge_call(layer, xs, ya, yb, gates, modt, wa, wb, wo, norm2_g, wrt, n_lat):
    b, t, d = xs.shape
    nt = t // TOKEN_BLOCK
    nlb = n_lat // TOKEN_BLOCK
    nh = ya.shape[1]
    ne = wrt.shape[1]
    head_spec = pl.BlockSpec((1, nh, TOKEN_BLOCK, LANES), lambda bi, ti: (bi, 0, ti, 0))
    w_spec = pl.BlockSpec((1, d, d), lambda bi, ti: (layer, 0, 0))
    tok_spec = pl.BlockSpec((1, TOKEN_BLOCK, d), lambda bi, ti: (bi, ti, 0))
    return pl.pallas_call(
        functools.partial(_merge_kernel, d=d),
        grid=(b, nt),
        in_specs=[tok_spec, head_spec, head_spec,
                  pl.BlockSpec((1, TOKEN_BLOCK, 2 * d), lambda bi, ti: (bi, ti, 0)),
                  pl.BlockSpec((1, 1, 6, d), lambda bi, ti: (bi, ti // nlb, 0, 0)),
                  w_spec, w_spec, w_spec,
                  pl.BlockSpec((1, 1, d), lambda bi, ti: (layer, 0, 0)),
                  pl.BlockSpec((1, ne, d), lambda bi, ti: (layer, 0, 0))],
        out_specs=[tok_spec, tok_spec,
                   pl.BlockSpec((1, 1, ne, TOKEN_BLOCK), lambda bi, ti: (bi, ti, 0, 0))],
        out_shape=[jax.ShapeDtypeStruct((b, t, d), F32), jax.ShapeDtypeStruct((b, t, d), BF16),
                   jax.ShapeDtypeStruct((b, nt, ne, TOKEN_BLOCK), F32)],
        compiler_params=_cparams(2),
        name="merge",
    )(xs, ya, yb, gates, modt, wa, wb, wo, norm2_g, wrt)


def _route_kernel(afft_ref, tri_ref, pos_ref, cnt_ref, *, nlb, cap_l, cap_c):
    nt = afft_ref.shape[1]
    ne = afft_ref.shape[2]
    tri = tri_ref[...]

    def bits(blk):
        return lax.bitcast_convert_type(afft_ref[0, blk], I32)

    def count(pred_fn, blocks):
        acc = jnp.zeros((ne, TOKEN_BLOCK), F32)
        for blk in blocks:
            acc = acc + jnp.where(pred_fn(bits(blk)), 1.0, 0.0)
        return jnp.sum(acc, axis=1, keepdims=True)

    def select(blocks, cap, base, with_counts):
        def it(k, thr):
            cand = thr | jnp.left_shift(jnp.int32(1), 30 - k)
            return jnp.where(count(lambda bt: bt >= cand, blocks) >= cap, cand, thr)
        thr = lax.fori_loop(0, 31, it, jnp.zeros((ne, 1), I32))
        ties_wanted = cap - count(lambda bt: bt > thr, blocks)
        seen_eq = jnp.zeros((ne, 1), F32)
        seen_sel = jnp.zeros((ne, 1), F32)
        lane = lax.broadcasted_iota(I32, (ne, LANES), 1)
        cntv = jnp.zeros((ne, LANES), F32)
        for n, blk in enumerate(blocks):
            bt = bits(blk)
            eq = bt == thr
            eqf = jnp.where(eq, 1.0, 0.0)
            rank = jnp.dot(eqf.astype(BF16), tri, preferred_element_type=F32) + seen_eq
            self_ = jnp.where(eq, jnp.where(rank < ties_wanted, 1.0, 0.0), jnp.where(bt > thr, 1.0, 0.0))
            sel = self_ > 0.5
            slot = jnp.dot(self_.astype(BF16), tri, preferred_element_type=F32) + seen_sel
            pos_ref[0, blk] = jnp.where(sel, slot.astype(I32) + base, -1)
            if with_counts:
                cntv = jnp.where(lane == n, seen_sel, cntv)
            seen_eq = seen_eq + jnp.sum(eqf, axis=1, keepdims=True)
            seen_sel = seen_sel + jnp.sum(self_, axis=1, keepdims=True)
        if with_counts:
            cntv = jnp.where(lane == len(blocks), seen_sel, cntv)
            cnt_ref[0] = cntv.astype(I32)

    select(list(range(nlb)), cap_l, 0, True)
    select(list(range(nlb, nt)), cap_c, cap_l, False)


def _route_call(afft, tri, n_lat, cap_l, cap_c):
    b, nt, ne, _ = afft.shape
    return pl.pallas_call(
        functools.partial(_route_kernel, nlb=n_lat // TOKEN_BLOCK, cap_l=cap_l, cap_c=cap_c),
        grid=(b,),
        in_specs=[pl.BlockSpec((1, nt, ne, TOKEN_BLOCK), lambda bi: (bi, 0, 0, 0)),
                  pl.BlockSpec((TOKEN_BLOCK, TOKEN_BLOCK), lambda bi: (0, 0))],
        out_specs=[pl.BlockSpec((1, nt, ne, TOKEN_BLOCK), lambda bi: (bi, 0, 0, 0)),
                   pl.BlockSpec((1, ne, LANES), lambda bi: (bi, 0, 0))],
        out_shape=[jax.ShapeDtypeStruct(afft.shape, I32), jax.ShapeDtypeStruct((b, ne, LANES), I32)],
        compiler_params=_cparams(1),
        name="route",
    )(afft, tri)


def _slot_onehot(pos_row, first_slot, n_slots):
    slots = first_slot + lax.broadcasted_iota(I32, (n_slots, pos_row.shape[1]), 0)
    return jnp.where(pos_row == slots, 1.0, 0.0).astype(BF16)


def _gather_kernel(cnt_ref, h2_ref, pos_ref, xe_ref, *, nlb, cap_l, cap_c):
    bi = pl.program_id(0)
    e = pl.program_id(1)
    ne = pl.num_programs(1)
    cbase = (bi * ne + e) * (nlb + 1)
    win = GATHER_WINDOW
    xe_ref[0, 0, :cap_l, :] = jnp.zeros((cap_l, xe_ref.shape[3]), BF16)

    def slot_range(blk):
        return cnt_ref[cbase + blk], cnt_ref[cbase + blk + 1]

    def window_start(lo):
        return jnp.minimum((lo // BF16_ROWS) * BF16_ROWS, cap_l - win)

    def add_rows(blk, t0, s0, n):
        onehot = _slot_onehot(pos_ref[0, blk, pl.ds(e, 1), :], s0, n)
        xe_ref[0, 0, pl.ds(s0, n), :] += jnp.dot(onehot, h2_ref[0, pl.ds(t0, TOKEN_BLOCK), :],
                                                 preferred_element_type=F32).astype(BF16)

    fits = jnp.bool_(True)
    for blk in range(nlb):
        lo, hi = slot_range(blk)
        fits = fits & (hi <= window_start(lo) + win)

    @pl.when(fits)
    def _():
        for blk in range(nlb):
            s0 = pl.multiple_of(window_start(slot_range(blk)[0]), BF16_ROWS)
            add_rows(blk, blk * TOKEN_BLOCK, s0, win)

    @pl.when(jnp.logical_not(fits))
    def _():
        def body(blk, _):
            lo, hi = slot_range(blk)

            def chunk(c, _):
                add_rows(blk, pl.multiple_of(blk * TOKEN_BLOCK, TOKEN_BLOCK),
                         pl.multiple_of(c * SLOT_CHUNK, SLOT_CHUNK), SLOT_CHUNK)
                return 0

            lax.fori_loop(lo // SLOT_CHUNK, jnp.where(hi > lo, (hi + SLOT_CHUNK - 1) // SLOT_CHUNK, 0), chunk, 0)
            return 0

        lax.fori_loop(0, nlb, body, 0)

    ctx_acc = jnp.zeros((cap_c, h2_ref.shape[2]), F32)
    for blk in range(nlb, pos_ref.shape[1]):
        onehot = _slot_onehot(pos_ref[0, blk, pl.ds(e, 1), :], cap_l, cap_c)
        t0 = blk * TOKEN_BLOCK
        ctx_acc = ctx_acc + jnp.dot(onehot, h2_ref[0, t0:t0 + TOKEN_BLOCK, :], preferred_element_type=F32)
    xe_ref[0, 0, cap_l:cap_l + cap_c, :] = ctx_acc.astype(BF16)


def _gather_call(cnt_flat, h2, pos, n_lat, cap_l, cap_c):
    b, t, d = h2.shape
    _, nt, ne, _ = pos.shape
    slots = cap_l + cap_c
    return pl.pallas_call(
        functools.partial(_gather_kernel, nlb=n_lat // TOKEN_BLOCK, cap_l=cap_l, cap_c=cap_c),
        grid_spec=pltpu.PrefetchScalarGridSpec(
            num_scalar_prefetch=1, grid=(b, ne),
            in_specs=[pl.BlockSpec((1, t, d), lambda bi, ei, s: (bi, 0, 0)),
                      pl.BlockSpec((1, nt, ne, TOKEN_BLOCK), lambda bi, ei, s: (bi, 0, 0, 0))],
            out_specs=pl.BlockSpec((1, 1, slots, d), lambda bi, ei, s: (bi, ei, 0, 0))),
        out_shape=jax.ShapeDtypeStruct((b, ne, slots, d), BF16),
        compiler_params=_cparams(2),
        name="moe_gather",
    )(cnt_flat, h2, pos)


def _ffn_kernel(x_ref, w1_ref, w3_ref, w2_ref, o_ref):
    x = x_ref[0, 0]
    ff = w1_ref.shape[-1]
    acc = None
    for f0 in range(0, ff, FF_CHUNK):
        f1 = min(ff, f0 + FF_CHUNK)
        a = jnp.dot(x, w1_ref[0, 0, :, f0:f1], preferred_element_type=F32)
        g = jnp.dot(x, w3_ref[0, 0, :, f0:f1], preferred_element_type=F32)
        hid = (a * jax.nn.sigmoid(a) * g).astype(BF16)
        part = jnp.dot(hid, w2_ref[0, 0, f0:f1, :], preferred_element_type=F32)
        acc = part if acc is None else acc + part
    o_ref[0, 0] = acc.astype(BF16)


def _ffn_call(layer, xe, w1, w3, w2):
    b, ne, slots, d = xe.shape
    ff = w1.shape[-1]
    x_spec = pl.BlockSpec((1, 1, slots, d), lambda ei, bi: (bi, ei, 0, 0))
    return pl.pallas_call(
        _ffn_kernel,
        grid=(ne, b),
        in_specs=[x_spec,
                  pl.BlockSpec((1, 1, d, ff), lambda ei, bi: (layer, ei, 0, 0)),
                  pl.BlockSpec((1, 1, d, ff), lambda ei, bi: (layer, ei, 0, 0)),
                  pl.BlockSpec((1, 1, ff, d), lambda ei, bi: (layer, ei, 0, 0))],
        out_specs=x_spec,
        out_shape=jax.ShapeDtypeStruct(xe.shape, BF16),
        compiler_params=_cparams(2),
        name="moe_ffn",
    )(xe, w1, w3, w2)


def _combine_kernel(cnt_ref, x_ref, ye_ref, pos_ref, afft_ref, mod_ref, xo_ref, acc_ref, *, nlb, cap_l, cap_c):
    bi = pl.program_id(0)
    ti = pl.program_id(1)
    ne = ye_ref.shape[1]
    is_lat = ti < nlb
    tn = (((0,), (0,)), ((), ()))
    win = min(2 * SLOT_CHUNK, cap_l)

    def slot_range(e):
        cbase = (bi * ne + e) * (nlb + 1) + jnp.minimum(ti, nlb - 1)
        return cnt_ref[cbase], cnt_ref[cbase + 1]

    def window_start(lo):
        return jnp.minimum((lo // SLOT_CHUNK) * SLOT_CHUNK, cap_l - win)

    def gated_onehot(e, first_slot, n_slots):
        slots = first_slot + lax.broadcasted_iota(I32, (n_slots, TOKEN_BLOCK), 0)
        return jnp.where(pos_ref[0, 0, e:e + 1, :] == slots, afft_ref[0, 0, e:e + 1, :], 0.0).astype(BF16)

    def residual(y):
        xo_ref[0] = x_ref[0] + mod_ref[0, 0][5:6] * y

    fits = is_lat
    for e in range(ne):
        lo, hi = slot_range(e)
        fits = fits & (hi <= window_start(lo) + win)

    @pl.when(fits)
    def _():
        acc = None
        for e in range(ne):
            s0 = pl.multiple_of(window_start(slot_range(e)[0]), SLOT_CHUNK)
            part = lax.dot_general(gated_onehot(e, s0, win), ye_ref[0, e, pl.ds(s0, win), :], tn,
                                   preferred_element_type=F32)
            acc = part if acc is None else acc + part
        residual(acc)

    @pl.when(is_lat & jnp.logical_not(fits))
    def _():
        acc_ref[...] = jnp.zeros_like(acc_ref)
        for e in range(ne):
            lo, hi = slot_range(e)

            def body(c, _, e=e):
                s0 = pl.multiple_of(c * SLOT_CHUNK, SLOT_CHUNK)
                acc_ref[...] += lax.dot_general(gated_onehot(e, s0, SLOT_CHUNK),
                                                ye_ref[0, e, pl.ds(s0, SLOT_CHUNK), :], tn,
                                                preferred_element_type=F32)
                return 0

            lax.fori_loop(lo // SLOT_CHUNK, jnp.where(hi > lo, (hi + SLOT_CHUNK - 1) // SLOT_CHUNK, 0), body, 0)
        residual(acc_ref[...])

    @pl.when(jnp.logical_not(is_lat))
    def _():
        acc = None
        for e in range(ne):
            part = lax.dot_general(gated_onehot(e, cap_l, cap_c), ye_ref[0, e, cap_l:cap_l + cap_c, :], tn,
                                   preferred_element_type=F32)
            acc = part if acc is None else acc + part
        residual(acc)


def _combine_call(cnt_flat, xs, ye, pos, afft, modt, n_lat, cap_l, cap_c):
    b, t, d = xs.shape
    _, nt, ne, _ = pos.shape
    nlb = n_lat // TOKEN_BLOCK
    slots = cap_l + cap_c
    tok_spec = pl.BlockSpec((1, TOKEN_BLOCK, d), lambda bi, ti, s: (bi, ti, 0))
    return pl.pallas_call(
        functools.partial(_combine_kernel, nlb=nlb, cap_l=cap_l, cap_c=cap_c),
        grid_spec=pltpu.PrefetchScalarGridSpec(
            num_scalar_prefetch=1, grid=(b, nt),
            in_specs=[tok_spec,
                      pl.BlockSpec((1, ne, slots, d), lambda bi, ti, s: (bi, 0, 0, 0)),
                      pl.BlockSpec((1, 1, ne, TOKEN_BLOCK), lambda bi, ti, s: (bi, ti, 0, 0)),
                      pl.BlockSpec((1, 1, ne, TOKEN_BLOCK), lambda bi, ti, s: (bi, ti, 0, 0)),
                      pl.BlockSpec((1, 1, 6, d), lambda bi, ti, s: (bi, ti // nlb, 0, 0))],
            out_specs=tok_spec,
            scratch_shapes=[pltpu.VMEM((TOKEN_BLOCK, d), F32)]),
        out_shape=jax.ShapeDtypeStruct(xs.shape, F32),
        compiler_params=_cparams(2),
        name="moe_combine",
    )(cnt_flat, xs, ye, pos, afft, modt)


def _rope_tables(n_lat, n_ctx):
    rows = n_lat // GRID_W
    row = jnp.repeat(jnp.arange(rows), GRID_W).astype(F32)
    col = jnp.tile(jnp.arange(GRID_W), rows).astype(F32)
    half = HEAD_DIM // 2
    inv_freq = ROPE_THETA ** (-jnp.arange(0, half, 2, dtype=F32) / half)
    ar = row[:, None] * inv_freq
    ac = col[:, None] * inv_freq
    cos64 = jnp.concatenate([jnp.cos(ar), jnp.cos(ar), jnp.cos(ac), jnp.cos(ac)], axis=1)
    sin64 = jnp.concatenate([-jnp.sin(ar), jnp.sin(ar), -jnp.sin(ac), jnp.sin(ac)], axis=1)
    cos_t = jnp.concatenate([jnp.tile(cos64, (1, LANES // HEAD_DIM)), jnp.ones((n_ctx, LANES), F32)], axis=0)
    sin_t = jnp.concatenate([jnp.tile(sin64, (1, LANES // HEAD_DIM)), jnp.zeros((n_ctx, LANES), F32)], axis=0)
    return cos_t, sin_t


def _swa_head_order():
    heads = []
    for j in range(SWA_KV_HEADS // 2):
        for g in range(SWA_GROUP):
            heads += [2 * SWA_GROUP * j + g, 2 * SWA_GROUP * j + SWA_GROUP + g]
    return np.concatenate([np.arange(h * HEAD_DIM, (h + 1) * HEAD_DIM) for h in heads])


def kernel(x, c, ctx, c_ctx, w_ada, b_ada, norm1_g, w_in, b_gate, diff_q_g, diff_k_g, diff_lambda, diff_subln_g,
           swa_q_g, swa_k_g, swa_sink, w_branch_a, w_branch_b, w_out, norm2_g, w_router, w_e1, w_e3, w_e2):
    b, n_lat, d = x.shape
    n_ctx = ctx.shape[1]
    depth = w_ada.shape[0]
    assert n_lat % TOKEN_BLOCK == 0 and n_ctx % TOKEN_BLOCK == 0 and n_lat >= 2 * TOKEN_BLOCK
    assert d == DIFF_HEADS * 2 * HEAD_DIM == SWA_Q_HEADS * HEAD_DIM and w_router.shape[-1] == N_EXPERTS
    cap_l = CAPACITY_FACTOR * n_lat // N_EXPERTS
    cap_c = CAPACITY_FACTOR * n_ctx // N_EXPERTS
    assert cap_l % SLOT_CHUNK == 0 and cap_c % 16 == 0
    nlb = n_lat // TOKEN_BLOCK

    perm = _swa_head_order()
    kvw = SWA_KV_HEADS * HEAD_DIM
    o = np.cumsum([0, d, d, d, d, kvw, kvw, d, d])
    w_in_p = jnp.concatenate(
        [w_in[..., o[0]:o[3]], w_in[..., o[3]:o[4]][..., perm], w_in[..., o[6]:o[8]], w_in[..., o[4]:o[6]]],
        axis=-1).astype(BF16)
    wa = w_branch_a.astype(BF16)
    wb = w_branch_b[:, perm, :].astype(BF16)
    wo = w_out.astype(BF16)
    wrt =jnp.swapaxes(w_router, 1, 2).astype(BF16)
    w1 = w_e1.astype(BF16)
    w3 = w_e3.astype(BF16)
    w2 = w_e2.astype(BF16)
    scale = HEAD_DIM ** -0.5
    hg = jnp.stack([jnp.tile(diff_q_g, (1, d // HEAD_DIM)) * (scale * math.log2(math.e)),
                    jnp.tile(diff_k_g, (1, d // HEAD_DIM)),
                    jnp.tile(swa_q_g, (1, d // HEAD_DIM)) * (scale * math.log2(math.e)),
                    jnp.tile(swa_k_g, (1, d // HEAD_DIM))], axis=1)
    def score_bound(q_gain, k_gain):
        return (HEAD_DIM * scale * math.log2(math.e)) * jnp.max(jnp.abs(q_gain), axis=1) * jnp.max(jnp.abs(k_gain), axis=1)

    diff_bounded = (score_bound(diff_q_g, diff_k_g) <= SCORE_BOUND_LOG2).astype(I32)
    swa_bounded = ((score_bound(swa_q_g, swa_k_g) <= SCORE_BOUND_LOG2)
                   & (jnp.max(swa_sink, axis=1) * math.log2(math.e) <= SCORE_BOUND_LOG2)).astype(I32)
    subln = diff_subln_g.reshape(depth, 1, LANES)
    g1n = norm1_g.reshape(depth, 1, d)
    g2n = norm2_g.reshape(depth, 1, d)
    bg = b_gate.reshape(depth, 1, 2 * d)
    gm = jnp.asarray(np.kron(np.eye(TOKEN_BLOCK // HEAD_DIM), np.ones((HEAD_DIM, HEAD_DIM))), BF16)
    tri = jnp.asarray(np.triu(np.ones((TOKEN_BLOCK, TOKEN_BLOCK)), 1), BF16)
    cos_t, sin_t = _rope_tables(n_lat, n_ctx)
    key_i = np.arange(2 * TOKEN_BLOCK)[:, None]
    qry_i = np.arange(TOKEN_BLOCK)[None, :]
    band = np.where(np.abs(qry_i + WINDOW - key_i) <= WINDOW, 0.0, NEG_INF)
    band = jnp.asarray(np.concatenate([band, np.zeros((n_ctx, TOKEN_BLOCK))], axis=0), BF16)

    rows = -(-(b + 1) // 8) * 8
    cc = jnp.concatenate([c, c_ctx[None], jnp.zeros((rows - b - 1, d), F32)], axis=0)
    mod_all = _ada_call(cc, w_ada, b_ada)

    xs = jnp.concatenate([x, ctx], axis=1)
    for i in range(depth):
        lam_init = 0.8 - 0.6 * math.exp(-0.3 * i)
        mod_l = mod_all[i, :b].reshape(b, 1, 6, d)
        mod_c = jnp.broadcast_to(mod_all[i, b].reshape(1, 1, 6, d), (b, 1, 6, d))
        modt = jnp.concatenate([mod_l, mod_c], axis=1)
        qa, ka, va, qb, gates, kb, vb = _inproj_call(i, xs, modt, g1n, w_in_p, gm, hg, bg, cos_t, sin_t, n_lat)
        ya = _diff_attn_call(i, diff_bounded[i:i + 1], qa, ka, va, diff_lambda, subln, n_lat, lam_init)
        yb = _swa_call(i, swa_bounded[i:i + 1], swa_sink[i], qb, kb, vb, band, n_lat)
        xs, h2, afft = _merge_call(i, xs, ya, yb, gates, modt, wa, wb, wo, g2n, wrt, n_lat)
        pos, cnt = _route_call(afft, tri, n_lat, cap_l, cap_c)
        cnt_flat = cnt[:, :, :nlb + 1].reshape(-1)
        xe = _gather_call(cnt_flat, h2, pos, n_lat, cap_l, cap_c)
        ye = _ffn_call(i, xe, w1, w3, w2)
        xs = _combine_call(cnt_flat, xs, ye, pos, afft, modt, n_lat, cap_l, cap_c)
    return xs[:, :n_lat]
```

```python
import functools
import math

import numpy as np
import jax
import jax.numpy as jnp
from jax import lax
from jax.experimental import pallas as pl
from jax.experimental.pallas import tpu as pltpu

F32 = jnp.float32
BF16 = jnp.bfloat16
I32 = jnp.int32

HEAD_DIM = 64
GRID_W = 64
DIFF_HEADS = 8
SWA_Q_HEADS = 16
SWA_KV_HEADS = 4
SWA_GROUP = SWA_Q_HEADS // SWA_KV_HEADS
WINDOW = 128
N_EXPERTS = 16
CAPACITY_FACTOR = 2
ROPE_THETA = 10000.0
EPS = 1e-6
NEG_INF = -1e30

LANES = 128
TOKEN_BLOCK = 256
SLOT_CHUNK = 128
BF16_ROWS = 16
SCORE_BOUND_LOG2 = 48.0
GATHER_WINDOW = 64
PROJ_CHUNK = 512
FF_CHUNK = 1536
VT_ROWS = LANES + 16
VMEM_LIMIT = 56 * 1024 * 1024


def _cparams(n_axes):
    return pltpu.CompilerParams(dimension_semantics=("arbitrary",) * n_axes, vmem_limit_bytes=VMEM_LIMIT)


def _ada_kernel(c_ref, w_ref, b_ref, o_ref):
    c = c_ref[...]
    sc = c * jax.nn.sigmoid(c)
    o_ref[0] = jnp.dot(sc, w_ref[0], preferred_element_type=F32) + b_ref[0]


def _ada_call(cc, w_ada, b_ada):
    depth, d, six_d = w_ada.shape
    rows = cc.shape[0]
    cols = 1536
    return pl.pallas_call(
        _ada_kernel,
        grid=(depth, six_d // cols),
        in_specs=[pl.BlockSpec((rows, d), lambda i, j: (0, 0)),
                  pl.BlockSpec((1, d, cols), lambda i, j: (i, 0, j)),
                  pl.BlockSpec((1, 1, cols), lambda i, j: (i, 0, j))],
        out_specs=pl.BlockSpec((1, rows, cols), lambda i, j: (i, 0, j)),
        out_shape=jax.ShapeDtypeStruct((depth, rows, six_d), F32),
        compiler_params=_cparams(2),
        name="ada",
    )(cc, w_ada, b_ada.reshape(depth, 1, six_d))


def _inproj_kernel(x_ref, mod_ref, g1_ref, w_ref, gm_ref, hg_ref, bg_ref, cos_ref, sin_ref,
                   qa_ref, ka_ref, va_ref, qb_ref, gate_ref, kb_ref, vb_ref, *, d):
    x = x_ref[0]
    mod = mod_ref[0, 0]
    ms = jnp.mean(x * x, axis=-1, keepdims=True)
    h = x * lax.rsqrt(ms + EPS) * g1_ref[0]
    h = (h * (1.0 + mod[1:2]) + mod[0:1]).astype(BF16)
    cos = cos_ref[...]
    sin = sin_ref[...]
    lane = lax.broadcasted_iota(I32, (1, LANES), 1)
    first_half = (lane % 32) < 16
    gm = gm_ref[...]

    def proj(c0, width):
        return jnp.dot(h, w_ref[0, :, c0:c0 + width], preferred_element_type=F32)

    def head_norm_rope(p, gain):
        w = p.shape[1]
        gw = gm.shape[0]
        sq = (p * p).astype(BF16)
        msq = jnp.concatenate([jnp.dot(sq[:, g0:g0 + gw], gm, preferred_element_type=F32)
                               for g0 in range(0, w, gw)], axis=1) * (1.0 / HEAD_DIM)
        qn = p * lax.rsqrt(msq + EPS) * gain
        outs = []
        for u in range(w // LANES):
            seg = qn[:, u * LANES:(u + 1) * LANES]
            partner = jnp.where(first_half, pltpu.roll(seg, LANES - 16, 1), pltpu.roll(seg, 16, 1))
            outs.append((seg * cos + partner * sin).astype(BF16))
        return outs

    per = PROJ_CHUNK // LANES
    for sec, (out_ref, gain_row) in enumerate(((qa_ref, 0), (ka_ref, 1))):
        for c in range(d // PROJ_CHUNK):
            c0 = sec * d + c * PROJ_CHUNK
            outs = head_norm_rope(proj(c0, PROJ_CHUNK), hg_ref[0, gain_row:gain_row + 1, c * PROJ_CHUNK:(c + 1) * PROJ_CHUNK])
            for u, o in enumerate(outs):
                out_ref[0, c * per + u] = o
    for c in range(d // PROJ_CHUNK):
        p = proj(2 * d + c * PROJ_CHUNK, PROJ_CHUNK)
        for u in range(per):
            va_ref[0, c * per + u, 0, :LANES, :] = p[:, u * LANES:(u + 1) * LANES].T.astype(BF16)
            va_ref[0, c * per + u, 0, LANES:, :] = jnp.ones((VT_ROWS - LANES, TOKEN_BLOCK), BF16)
    for c in range(d // PROJ_CHUNK):
        outs = head_norm_rope(proj(3 * d + c * PROJ_CHUNK, PROJ_CHUNK), hg_ref[0, 2:3, c * PROJ_CHUNK:(c + 1) * PROJ_CHUNK])
        for u, o in enumerate(outs):
            qb_ref[0, c * per + u] = o
    for c in range(2 * d // PROJ_CHUNK):
        p = proj(4 * d + c * PROJ_CHUNK, PROJ_CHUNK) + bg_ref[0, :, c * PROJ_CHUNK:(c + 1) * PROJ_CHUNK]
        gate_ref[0, :, c * PROJ_CHUNK:(c + 1) * PROJ_CHUNK] = jax.nn.sigmoid(p).astype(BF16)
    kvw = SWA_KV_HEADS * HEAD_DIM
    outs = head_norm_rope(proj(6 * d, kvw), hg_ref[0, 3:4, :kvw])
    for u, o in enumerate(outs):
        kb_ref[0, u] = o
    p = proj(6 * d + kvw, kvw)
    for u in range(kvw // LANES):
        vb_ref[0, u, 0, :LANES, :] = p[:, u * LANES:(u + 1) * LANES].T.astype(BF16)
        vb_ref[0, u, 0, LANES:, :] = jnp.ones((VT_ROWS - LANES, TOKEN_BLOCK), BF16)


def _inproj_call(layer, xs, modt, norm1_g, w_in_p, gm, hg, bg, cos_t, sin_t, n_lat):
    b, t, d = xs.shape
    nt = t // TOKEN_BLOCK
    nlb = n_lat // TOKEN_BLOCK
    in_w = w_in_p.shape[-1]
    nh = d // LANES
    nkv = SWA_KV_HEADS * HEAD_DIM // LANES
    head_shape = jax.ShapeDtypeStruct((b, nh, t, LANES), BF16)
    kv_shape = jax.ShapeDtypeStruct((b, nkv, t, LANES), BF16)
    head_spec = pl.BlockSpec((1, nh, TOKEN_BLOCK, LANES), lambda bi, ti: (bi, 0, ti, 0))
    kv_spec = pl.BlockSpec((1, nkv, TOKEN_BLOCK, LANES), lambda bi, ti: (bi, 0, ti, 0))
    vt_shape = jax.ShapeDtypeStruct((b, nh, nt, VT_ROWS, TOKEN_BLOCK), BF16)
    vt_spec = pl.BlockSpec((1, nh, 1, VT_ROWS, TOKEN_BLOCK), lambda bi, ti: (bi, 0, ti, 0, 0))
    kvt_shape = jax.ShapeDtypeStruct((b, nkv, nt, VT_ROWS, TOKEN_BLOCK), BF16)
    kvt_spec = pl.BlockSpec((1, nkv, 1, VT_ROWS, TOKEN_BLOCK), lambda bi, ti: (bi, 0, ti, 0, 0))
    return pl.pallas_call(
        functools.partial(_inproj_kernel, d=d),
        grid=(b, nt),
        in_specs=[pl.BlockSpec((1, TOKEN_BLOCK, d), lambda bi, ti: (bi, ti, 0)),
                  pl.BlockSpec((1, 1, 6, d), lambda bi, ti: (bi, ti // nlb, 0, 0)),
                  pl.BlockSpec((1, 1, d), lambda bi, ti: (layer, 0, 0)),
                  pl.BlockSpec((1, d, in_w), lambda bi, ti: (layer, 0, 0)),
                  pl.BlockSpec(gm.shape, lambda bi, ti: (0, 0)),
                  pl.BlockSpec((1, 4, d), lambda bi, ti: (layer, 0, 0)),
                  pl.BlockSpec((1, 1, 2 * d), lambda bi, ti: (layer, 0, 0)),
                  pl.BlockSpec((TOKEN_BLOCK, LANES), lambda bi, ti: (ti, 0)),
                  pl.BlockSpec((TOKEN_BLOCK, LANES), lambda bi, ti: (ti, 0))],
        out_specs=[head_spec, head_spec, vt_spec, head_spec,
                   pl.BlockSpec((1, TOKEN_BLOCK, 2 * d), lambda bi, ti: (bi, ti, 0)),
                   kv_spec, kvt_spec],
        out_shape=[head_shape, head_shape, vt_shape, head_shape,
                   jax.ShapeDtypeStruct((b, t, 2 * d), BF16), kv_shape, kvt_shape],
        compiler_params=_cparams(2),
        name="inproj",
    )(xs, modt, norm1_g, w_in_p, gm, hg, bg, cos_t, sin_t)


def _diff_attn_kernel(flag_ref, q_ref, k_ref, v_ref, dl_ref, sg_ref, *rest, key_blocks, bpu, lam_init):
    o_ref = rest[-1]
    q = q_ref[0, 0]
    lane = lax.broadcasted_iota(I32, q.shape, 1)
    zero = jnp.zeros_like(q)
    q2 = jnp.concatenate([jnp.where(lane < HEAD_DIM, q, zero), jnp.where(lane >= HEAD_DIM, q, zero)], axis=0)
    nq = q2.shape[0]
    nn_t = (((1,), (1,)), ((), ()))

    def scores(blk):
        k = k_ref[0, 0, blk * TOKEN_BLOCK:(blk + 1) * TOKEN_BLOCK, :]
        return lax.dot_general(k, q2, nn_t, preferred_element_type=F32).astype(BF16)

    def process(blocks, s_blocks, state, next_blocks):
        m, l, acc = state
        m_blk = s_blocks[0]
        for s in s_blocks[1:]:
            m_blk = jnp.maximum(m_blk, s)
        m_new = jnp.maximum(m, jnp.max(m_blk, axis=0, keepdims=True).astype(F32))
        alpha = jnp.exp2(m - m_new)
        m_bf = m_new.astype(BF16)
        s_next = []
        r = None
        for c, (blk, s) in enumerate(zip(blocks, s_blocks)):
            if c < len(next_blocks):
                s_next.append(scores(next_blocks[c]))
            p = jnp.exp2(s - m_bf)
            part = jnp.dot(v_ref[0, 0, blk], p, preferred_element_type=F32)
            r = part if r is None else r + part
        s_next += [scores(blk) for blk in next_blocks[len(blocks):]]
        return (m_new, alpha * l + r[LANES:LANES + 1], alpha * acc + r[:LANES]), s_next

    init = (jnp.full((1, nq), NEG_INF, F32), jnp.zeros((1, nq), F32), jnp.zeros((LANES, nq), F32))

    def finish(state):
        _, l, acc = state
        o = acc / l
        dl = dl_ref[0]
        lam = (jnp.exp(jnp.sum(dl[0:1] * dl[1:2], axis=1, keepdims=True))
               - jnp.exp(jnp.sum(dl[2:3] * dl[3:4], axis=1, keepdims=True)) + lam_init)
        y = o[:, :nq // 2] - lam * o[:, nq // 2:]
        y = y * lax.rsqrt(jnp.mean(y * y, axis=0, keepdims=True) + EPS) * (sg_ref[0][:, 0:1] * (1.0 - lam_init))
        o_ref[0, 0] = y.astype(BF16)

    def bounded(blocks):
        r = None
        s = scores(blocks[0])
        for n, blk in enumerate(blocks):
            s_next = scores(blocks[n + 1]) if n + 1 < len(blocks) else None
            part = jnp.dot(v_ref[0, 0, blk], jnp.exp2(s), preferred_element_type=F32)
            r = part if r is None else r + part
            s = s_next
        finish((None, r[LANES:LANES + 1], r[:LANES]))

    @pl.when(flag_ref[0] > 0)
    def _():
        bounded(key_blocks)

    @pl.when(flag_ref[0] <= 0)
    def _():
        units = [key_blocks[u:u + bpu] for u in range(0, len(key_blocks), bpu)]
        state = init
        s_cur = [scores(blk) for blk in units[0]]
        for u, blocks in enumerate(units):
            state, s_cur = process(blocks, s_cur, state, units[u + 1] if u + 1 < len(units) else [])
        finish(state)


def _diff_attn_call(layer, flag, qa, ka, vat, diff_lambda, subln_g, n_lat, lam_init):
    b, nh, t, _ = qa.shape
    nt = t // TOKEN_BLOCK
    nlb = n_lat // TOKEN_BLOCK
    n_ctx = t - n_lat
    assert n_lat % n_ctx == 0
    tq = next(u for u in (2 * TOKEN_BLOCK, TOKEN_BLOCK) if n_lat % u == 0)
    k_spec = pl.BlockSpec((1, 1, t, LANES), lambda bi, hi, ti, f: (bi, hi, 0, 0))
    v_spec = pl.BlockSpec((1, 1, nt, VT_ROWS, TOKEN_BLOCK), lambda bi, hi, ti, f: (bi, hi, 0, 0, 0))
    par_specs = [pl.BlockSpec((1, 4, HEAD_DIM), lambda bi, hi, ti, f: (layer, 0, 0)),
                 pl.BlockSpec((1, LANES, LANES), lambda bi, hi, ti, f: (layer, 0, 0))]

    def call(rows, first_block, n_steps, key_blocks, prev):
        q_spec = pl.BlockSpec((1, 1, rows, LANES), lambda bi, hi, ti, f: (bi, hi, first_block + ti, 0))
        o_spec = pl.BlockSpec((1, 1, LANES, rows), lambda bi, hi, ti, f: (bi, hi, 0, first_block + ti))
        extra = [] if prev is None else [pl.BlockSpec(memory_space=pl.ANY)]
        operands = (flag, qa, ka, vat, diff_lambda, subln_g) + (() if prev is None else (prev,))
        return pl.pallas_call(
            functools.partial(_diff_attn_kernel, key_blocks=key_blocks, bpu=4, lam_init=lam_init),
            grid_spec=pltpu.PrefetchScalarGridSpec(
                num_scalar_prefetch=1, grid=(b, nh, n_steps),
                in_specs=[q_spec, k_spec, v_spec] + par_specs + extra, out_specs=o_spec),
            out_shape=jax.ShapeDtypeStruct((b, nh, LANES, t), BF16),
            input_output_aliases={} if prev is None else {len(operands) - 1: 0},
            compiler_params=_cparams(3),
            name="diff_attn",
        )(*operands)

    ya = call(tq, 0, n_lat // tq, list(range(nt)), None)
    return call(n_ctx, n_lat // n_ctx, 1, list(range(nlb, nt)), ya)


def _swa_kernel(flag_ref, sink_ref, q_ref, k_ref, v_ref, band_ref, o_ref, *, nlb):
    j = pl.program_id(1)
    i = pl.program_id(2)
    nt = v_ref.shape[2]
    n_lat = nlb * TOKEN_BLOCK
    is_lat = i < nlb
    im1 = jnp.maximum(i - 1, 0)
    ip1 = jnp.minimum(i + 1, nt - 1)

    def krows(blk, lo, n):
        return k_ref[0, 0, pl.ds(pl.multiple_of(blk * TOKEN_BLOCK + lo, WINDOW), n), :]

    kcat = jnp.concatenate([krows(im1, WINDOW, WINDOW), krows(i, 0, TOKEN_BLOCK), krows(ip1, 0, WINDOW),
                            k_ref[0, 0, n_lat:, :]], axis=0)
    vcat = jnp.concatenate([v_ref[0, 0, im1][:, WINDOW:], v_ref[0, 0, i], v_ref[0, 0, ip1][:, :WINDOW]]
                           + [v_ref[0, 0, blk] for blk in range(nlb, nt)], axis=1)
    nk = kcat.shape[0]
    neg = jnp.float32(NEG_INF)
    zero = jnp.float32(0.0)
    pen = (jnp.where((i == 0) | jnp.logical_not(is_lat), neg, zero), jnp.where(is_lat, zero, neg),
           jnp.where(i >= nlb - 1, neg, zero))
    part = jnp.concatenate([jnp.full((WINDOW, TOKEN_BLOCK), pen[0], F32), jnp.full((TOKEN_BLOCK, TOKEN_BLOCK), pen[1], F32),
                            jnp.full((WINDOW, TOKEN_BLOCK), pen[2], F32),
                            jnp.zeros((nk - 2 * TOKEN_BLOCK, TOKEN_BLOCK), F32)], axis=0)
    bias1 = band_ref[...] + part.astype(BF16)
    bias = jnp.concatenate([bias1] * SWA_GROUP, axis=1)
    lane = lax.broadcasted_iota(I32, (TOKEN_BLOCK, LANES), 1)
    nn_t = (((1,), (1,)), ((), ()))
    log2e = math.log2(math.e)

    def queries(half):
        in_half = (lane >= half * HEAD_DIM) & (lane < (half + 1) * HEAD_DIM)
        return jnp.concatenate([jnp.where(in_half, q_ref[0, g], jnp.zeros((TOKEN_BLOCK, LANES), BF16))
                                for g in range(SWA_GROUP)], axis=0)

    def scores(qs, k0, k1):
        return (lax.dot_general(kcat[k0:k1], qs, nn_t, preferred_element_type=F32).astype(BF16) + bias[k0:k1])

    def sinks(half):
        return jnp.concatenate(
            [jnp.full((1, TOKEN_BLOCK), sink_ref[2 * SWA_GROUP * j + SWA_GROUP * half + g] * log2e, F32)
             for g in range(SWA_GROUP)], axis=1)

    def head_values(half):
        return jnp.concatenate([vcat[half * HEAD_DIM:(half + 1) * HEAD_DIM], vcat[LANES:]], axis=0)

    def store(o0, o1):
        for g in range(SWA_GROUP):
            sl = slice(g * TOKEN_BLOCK, (g + 1) * TOKEN_BLOCK)
            o_ref[0, g] = jnp.concatenate([o0[:, sl], o1[:, sl]], axis=0).astype(BF16)

    @pl.when(flag_ref[0] > 0)
    def _():
        steps = [(half, k0) for half in range(2) for k0 in range(0, nk, TOKEN_BLOCK)]
        qs = [queries(0), queries(1)]
        vh = [head_values(0), head_values(1)]
        r = [None, None]
        s = scores(qs[0], 0, TOKEN_BLOCK)
        for n, (half, k0) in enumerate(steps):
            s_next = None
            if n + 1 < len(steps):
                h2, k2 = steps[n + 1]
                s_next = scores(qs[h2], k2, k2 + TOKEN_BLOCK)
            part = jnp.dot(vh[half][:, k0:k0 + TOKEN_BLOCK], jnp.exp2(s), preferred_element_type=F32)
            r[half] = part if r[half] is None else r[half] + part
            s = s_next
        store(*[r[h][:HEAD_DIM] / (r[h][HEAD_DIM:HEAD_DIM + 1] + jnp.exp2(sinks(h))) for h in range(2)])

    @pl.when(flag_ref[0] <= 0)
    def _():
        s_all = [scores(queries(half), 0, nk) for half in range(2)]
        outs = []
        for half, s in enumerate(s_all):
            sink = sinks(half)
            m = jnp.maximum(jnp.max(s, axis=0, keepdims=True).astype(F32), sink).astype(BF16)
            r = jnp.dot(head_values(half), jnp.exp2(s - m), preferred_element_type=F32)
            outs.append(r[:HEAD_DIM] / (r[HEAD_DIM:HEAD_DIM + 1] + jnp.exp2(sink - m.astype(F32))))
        store(*outs)


def _swa_call(layer, flag, sink, qb, kb, vbt, band, n_lat):
    b, nslab, t, _ = qb.shape
    nt = t // TOKEN_BLOCK
    npair = kb.shape[1]
    assert 2 * WINDOW == TOKEN_BLOCK
    q_spec = pl.BlockSpec((1, SWA_GROUP, TOKEN_BLOCK, LANES), lambda bi, ji, ti, f, s: (bi, ji, ti, 0))
    k_spec = pl.BlockSpec((1, 1, t, LANES), lambda bi, ji, ti, f, s: (bi, ji, 0, 0))
    v_spec = pl.BlockSpec((1, 1, nt, VT_ROWS, TOKEN_BLOCK), lambda bi, ji, ti, f, s: (bi, ji, 0, 0, 0))
    return pl.pallas_call(
        functools.partial(_swa_kernel, nlb=n_lat // TOKEN_BLOCK),
        grid_spec=pltpu.PrefetchScalarGridSpec(
            num_scalar_prefetch=2, grid=(b, npair, nt),
            in_specs=[q_spec, k_spec, v_spec, pl.BlockSpec(band.shape, lambda bi, ji, ti, f, s: (0, 0))],
            out_specs=pl.BlockSpec((1, SWA_GROUP, LANES, TOKEN_BLOCK), lambda bi, ji, ti, f, s: (bi, ji, 0, ti))),
        out_shape=jax.ShapeDtypeStruct((b, nslab, LANES, t), BF16),
        compiler_params=_cparams(3),
        name="swa_attn",
    )(flag, sink, qb, kb, vbt, band)


def _merge_kernel(x_ref, ya_ref, yb_ref, gate_ref, mod_ref, wa_ref, wb_ref, wo_ref, g2_ref, wrt_ref,
                  xo_ref, h2_ref, afft_ref, *, d):
    nh = ya_ref.shape[1]
    tn = (((0,), (0,)), ((), ()))
    ya = jnp.concatenate([ya_ref[0, h] for h in range(nh)], axis=0)
    yb = jnp.concatenate([yb_ref[0, h] for h in range(nh)], axis=0)
    za = lax.dot_general(ya, wa_ref[0], tn, preferred_element_type=F32)
    zb = lax.dot_general(yb, wb_ref[0], tn, preferred_element_type=F32)
    gate = gate_ref[0].astype(F32)
    u = gate[:, :d] * za + gate[:, d:] * zb
    z = jnp.dot(u.astype(BF16), wo_ref[0], preferred_element_type=F32)
    mod = mod_ref[0, 0]
    xn = x_ref[0] + mod[2:3] * z
    xo_ref[0] = xn
    ms = jnp.mean(xn * xn, axis=-1, keepdims=True)
    h2 = xn * lax.rsqrt(ms + EPS) * g2_ref[0]
    h2 = (h2 * (1.0 + mod[4:5]) + mod[3:4]).astype(BF16)
    h2_ref[0] = h2
    lgt = lax.dot_general(wrt_ref[0], h2, (((1,), (1,)), ((), ())), preferred_element_type=F32)
    et = jnp.exp(lgt - jnp.max(lgt, axis=0, keepdims=True))
    afft_ref[0, 0] = et / jnp.sum(et, axis=0, keepdims=True)


def _merge_call(layer, xs, ya, yb, gates, modt, wa, wb, wo, norm2_g, wrt, n_lat):
    b, t, d = xs.shape
    nt = t // TOKEN_BLOCK
    nlb = n_lat // TOKEN_BLOCK
    nh = ya.shape[1]
    ne = wrt.shape[1]
    head_spec = pl.BlockSpec((1, nh, LANES, TOKEN_BLOCK), lambda bi, ti: (bi, 0, 0, ti))
    w_spec = pl.BlockSpec((1, d, d), lambda bi, ti: (layer, 0, 0))
    tok_spec = pl.BlockSpec((1, TOKEN_BLOCK, d), lambda bi, ti: (bi, ti, 0))
    return pl.pallas_call(
        functools.partial(_merge_kernel, d=d),
        grid=(b, nt),
        in_specs=[tok_spec, head_spec, head_spec,
                  pl.BlockSpec((1, TOKEN_BLOCK, 2 * d), lambda bi, ti: (bi, ti, 0)),
                  pl.BlockSpec((1, 1, 6, d), lambda bi, ti: (bi, ti // nlb, 0, 0)),
                  w_spec, w_spec, w_spec,
                  pl.BlockSpec((1, 1, d), lambda bi, ti: (layer, 0, 0)),
                  pl.BlockSpec((1, ne, d), lambda bi, ti: (layer, 0, 0))],
        out_specs=[tok_spec, tok_spec,
                   pl.BlockSpec((1, 1, ne, TOKEN_BLOCK), lambda bi, ti: (bi, ti, 0, 0))],
        out_shape=[jax.ShapeDtypeStruct((b, t, d), F32), jax.ShapeDtypeStruct((b, t, d), BF16),
                   jax.ShapeDtypeStruct((b, nt, ne, TOKEN_BLOCK), F32)],
        compiler_params=_cparams(2),
        name="merge",
    )(xs, ya, yb, gates, modt, wa, wb, wo, norm2_g, wrt)


def _route_kernel(afft_ref, tri_ref, pos_ref, cnt_ref, *, nlb, cap_l, cap_c):
    nt = afft_ref.shape[1]
    ne = afft_ref.shape[2]
    tri = tri_ref[...]

    def bits(blk):
        return lax.bitcast_convert_type(afft_ref[0, blk], I32)

    def count(pred_fn, blocks):
        acc = jnp.zeros((ne, TOKEN_BLOCK), F32)
        for blk in blocks:
            acc = acc + jnp.where(pred_fn(bits(blk)), 1.0, 0.0)
        return jnp.sum(acc, axis=1, keepdims=True)

    def select(blocks, cap, base, with_counts):
        def it(k, thr):
            cand = thr | jnp.left_shift(jnp.int32(1), 30 - k)
            return jnp.where(count(lambda bt: bt >= cand, blocks) >= cap, cand, thr)
        thr = lax.fori_loop(0, 31, it, jnp.zeros((ne, 1), I32))
        ties_wanted = cap - count(lambda bt: bt > thr, blocks)
        seen_eq = jnp.zeros((ne, 1), F32)
        seen_sel = jnp.zeros((ne, 1), F32)
        lane = lax.broadcasted_iota(I32, (ne, LANES), 1)
        cntv = jnp.zeros((ne, LANES), F32)
        for n, blk in enumerate(blocks):
            bt = bits(blk)
            eq = bt == thr
            eqf = jnp.where(eq, 1.0, 0.0)
            rank = jnp.dot(eqf.astype(BF16), tri, preferred_element_type=F32) + seen_eq
            self_ = jnp.where(eq, jnp.where(rank < ties_wanted, 1.0, 0.0), jnp.where(bt > thr, 1.0, 0.0))
            sel = self_ > 0.5
            slot = jnp.dot(self_.astype(BF16), tri, preferred_element_type=F32) + seen_sel
            pos_ref[0, blk] = jnp.where(sel, slot.astype(I32) + base, -1)
            if with_counts:
                cntv = jnp.where(lane == n, seen_sel, cntv)
            seen_eq = seen_eq + jnp.sum(eqf, axis=1, keepdims=True)
            seen_sel = seen_sel + jnp.sum(self_, axis=1, keepdims=True)
        if with_counts:
            cntv = jnp.where(lane == len(blocks), seen_sel, cntv)
            cnt_ref[0] = cntv.astype(I32)

    select(list(range(nlb)), cap_l, 0, True)
    select(list(range(nlb, nt)), cap_c, cap_l, False)


def _route_call(afft, tri, n_lat, cap_l, cap_c):
    b, nt, ne, _ = afft.shape
    return pl.pallas_call(
        functools.partial(_route_kernel, nlb=n_lat // TOKEN_BLOCK, cap_l=cap_l, cap_c=cap_c),
        grid=(b,),
        in_specs=[pl.BlockSpec((1, nt, ne, TOKEN_BLOCK), lambda bi: (bi, 0, 0, 0)),
                  pl.BlockSpec((TOKEN_BLOCK, TOKEN_BLOCK), lambda bi: (0, 0))],
        out_specs=[pl.BlockSpec((1, nt, ne, TOKEN_BLOCK), lambda bi: (bi, 0, 0, 0)),
                   pl.BlockSpec((1, ne, LANES), lambda bi: (bi, 0, 0))],
        out_shape=[jax.ShapeDtypeStruct(afft.shape, I32), jax.ShapeDtypeStruct((b, ne, LANES), I32)],
        compiler_params=_cparams(1),
        name="route",
    )(afft, tri)


def _slot_onehot(pos_row, first_slot, n_slots):
    slots = first_slot + lax.broadcasted_iota(I32, (n_slots, pos_row.shape[1]), 0)
    return jnp.where(pos_row == slots, 1.0, 0.0).astype(BF16)


def _gather_kernel(cnt_ref, h2_ref, pos_ref, xe_ref, *, nlb, cap_l, cap_c):
    bi = pl.program_id(0)
    e = pl.program_id(1)
    ne = pl.num_programs(1)
    cbase = (bi * ne + e) * (nlb + 1)
    win = GATHER_WINDOW
    xe_ref[0, 0, :cap_l, :] = jnp.zeros((cap_l, xe_ref.shape[3]), BF16)

    def slot_range(blk):
        return cnt_ref[cbase + blk], cnt_ref[cbase + blk + 1]

    def window_start(lo):
        return jnp.minimum((lo // BF16_ROWS) * BF16_ROWS, cap_l - win)

    def add_rows(blk, t0, s0, n):
        onehot = _slot_onehot(pos_ref[0, blk, pl.ds(e, 1), :], s0, n)
        xe_ref[0, 0, pl.ds(s0, n), :] += jnp.dot(onehot, h2_ref[0, pl.ds(t0, TOKEN_BLOCK), :],
                                                 preferred_element_type=F32).astype(BF16)

    fits = jnp.bool_(True)
    for blk in range(nlb):
        lo, hi = slot_range(blk)
        fits = fits & (hi <= window_start(lo) + win)

    @pl.when(fits)
    def _():
        for blk in range(nlb):
            s0 = pl.multiple_of(window_start(slot_range(blk)[0]), BF16_ROWS)
            add_rows(blk, blk * TOKEN_BLOCK, s0, win)

    @pl.when(jnp.logical_not(fits))
    def _():
        def body(blk, _):
            lo, hi = slot_range(blk)

            def chunk(c, _):
                add_rows(blk, pl.multiple_of(blk * TOKEN_BLOCK, TOKEN_BLOCK),
                         pl.multiple_of(c * SLOT_CHUNK, SLOT_CHUNK), SLOT_CHUNK)
                return 0

            lax.fori_loop(lo // SLOT_CHUNK, jnp.where(hi > lo, (hi + SLOT_CHUNK - 1) // SLOT_CHUNK, 0), chunk, 0)
            return 0

        lax.fori_loop(0, nlb, body, 0)

    ctx_acc = jnp.zeros((cap_c, h2_ref.shape[2]), F32)
    for blk in range(nlb, pos_ref.shape[1]):
        onehot = _slot_onehot(pos_ref[0, blk, pl.ds(e, 1), :], cap_l, cap_c)
        t0 = blk * TOKEN_BLOCK
        ctx_acc = ctx_acc + jnp.dot(onehot, h2_ref[0, t0:t0 + TOKEN_BLOCK, :], preferred_element_type=F32)
    xe_ref[0, 0, cap_l:cap_l + cap_c, :] = ctx_acc.astype(BF16)


def _gather_call(cnt_flat, h2, pos, n_lat, cap_l, cap_c):
    b, t, d = h2.shape
    _, nt, ne, _ = pos.shape
    slots = cap_l + cap_c
    return pl.pallas_call(
        functools.partial(_gather_kernel, nlb=n_lat // TOKEN_BLOCK, cap_l=cap_l, cap_c=cap_c),
        grid_spec=pltpu.PrefetchScalarGridSpec(
            num_scalar_prefetch=1, grid=(b, ne),
            in_specs=[pl.BlockSpec((1, t, d), lambda bi, ei, s: (bi, 0, 0)),
                      pl.BlockSpec((1, nt, ne, TOKEN_BLOCK), lambda bi, ei, s: (bi, 0, 0, 0))],
            out_specs=pl.BlockSpec((1, 1, slots, d), lambda bi, ei, s: (bi, ei, 0, 0))),
        out_shape=jax.ShapeDtypeStruct((b, ne, slots, d), BF16),
        compiler_params=_cparams(2),
        name="moe_gather",
    )(cnt_flat, h2, pos)


def _ffn_kernel(x_ref, w1_ref, w3_ref, w2_ref, o_ref):
    x = x_ref[0, 0]
    ff = w1_ref.shape[-1]
    acc = None
    for f0 in range(0, ff, FF_CHUNK):
        f1 = min(ff, f0 + FF_CHUNK)
        a = jnp.dot(x, w1_ref[0, 0, :, f0:f1], preferred_element_type=F32)
        g = jnp.dot(x, w3_ref[0, 0, :, f0:f1], preferred_element_type=F32)
        hid = (a * jax.nn.sigmoid(a) * g).astype(BF16)
        part = jnp.dot(hid, w2_ref[0, 0, f0:f1, :], preferred_element_type=F32)
        acc = part if acc is None else acc + part
    o_ref[0, 0] = acc.astype(BF16)


def _ffn_call(layer, xe, w1, w3, w2):
    b, ne, slots, d = xe.shape
    ff = w1.shape[-1]
    x_spec = pl.BlockSpec((1, 1, slots, d), lambda ei, bi: (bi, ei, 0, 0))
    return pl.pallas_call(
        _ffn_kernel,
        grid=(ne, b),
        in_specs=[x_spec,
                  pl.BlockSpec((1, 1, d, ff), lambda ei, bi: (layer, ei, 0, 0)),
                  pl.BlockSpec((1, 1, d, ff), lambda ei, bi: (layer, ei, 0, 0)),
                  pl.BlockSpec((1, 1, ff, d), lambda ei, bi: (layer, ei, 0, 0))],
        out_specs=x_spec,
        out_shape=jax.ShapeDtypeStruct(xe.shape, BF16),
        compiler_params=_cparams(2),
        name="moe_ffn",
    )(xe, w1, w3, w2)


def _combine_kernel(cnt_ref, x_ref, ye_ref, pos_ref, afft_ref, mod_ref, xo_ref, acc_ref, *, nlb, cap_l, cap_c):
    bi = pl.program_id(0)
    ti = pl.program_id(1)
    ne = ye_ref.shape[1]
    is_lat = ti < nlb
    tn = (((0,), (0,)), ((), ()))
    win = GATHER_WINDOW

    def slot_range(e):
        cbase = (bi * ne + e) * (nlb + 1) + jnp.minimum(ti, nlb - 1)
        return cnt_ref[cbase], cnt_ref[cbase + 1]

    def window_start(lo):
        return jnp.minimum((lo // BF16_ROWS) * BF16_ROWS, cap_l - win)

    def gated_onehot(e, first_slot, n_slots):
        slots = first_slot + lax.broadcasted_iota(I32, (n_slots, TOKEN_BLOCK), 0)
        return jnp.where(pos_ref[0, 0, e:e + 1, :] == slots, afft_ref[0, 0, e:e + 1, :], 0.0).astype(BF16)

    def residual(y):
        xo_ref[0] = x_ref[0] + mod_ref[0, 0][5:6] * y

    fits = is_lat
    for e in range(ne):
        lo, hi = slot_range(e)
        fits = fits & (hi <= window_start(lo) + win)

    def scatter(pieces):
        w = jnp.concatenate([p[0] for p in pieces], axis=0)
        y = jnp.concatenate([p[1] for p in pieces], axis=0)
        residual(lax.dot_general(w, y, tn, preferred_element_type=F32))

    @pl.when(fits)
    def _():
        pieces = []
        for e in range(ne):
            s0 = pl.multiple_of(window_start(slot_range(e)[0]), BF16_ROWS)
            pieces.append((gated_onehot(e, s0, win), ye_ref[0, e, pl.ds(s0, win), :]))
        scatter(pieces)

    @pl.when(is_lat & jnp.logical_not(fits))
    def _():
        acc_ref[...] = jnp.zeros_like(acc_ref)
        for e in range(ne):
            lo, hi = slot_range(e)

            def body(c, _, e=e):
                s0 = pl.multiple_of(c * SLOT_CHUNK, SLOT_CHUNK)
                acc_ref[...] += lax.dot_general(gated_onehot(e, s0, SLOT_CHUNK),
                                                ye_ref[0, e, pl.ds(s0, SLOT_CHUNK), :], tn,
                                                preferred_element_type=F32)
                return 0

            lax.fori_loop(lo // SLOT_CHUNK, jnp.where(hi > lo, (hi + SLOT_CHUNK - 1) // SLOT_CHUNK, 0), body, 0)
        residual(acc_ref[...])

    @pl.when(jnp.logical_not(is_lat))
    def _():
        scatter([(gated_onehot(e, cap_l, cap_c), ye_ref[0, e, cap_l:cap_l + cap_c, :]) for e in range(ne)])


def _combine_call(cnt_flat, xs, ye, pos, afft, modt, n_lat, cap_l, cap_c, latent_only):
    b, t, d = xs.shape
    _, nt, ne, _ = pos.shape
    nlb = n_lat // TOKEN_BLOCK
    slots = cap_l + cap_c
    n_steps, out_rows = (nlb, n_lat) if latent_only else (nt, t)
    tok_spec = pl.BlockSpec((1, TOKEN_BLOCK, d), lambda bi, ti, s: (bi, ti, 0))
    return pl.pallas_call(
        functools.partial(_combine_kernel, nlb=nlb, cap_l=cap_l, cap_c=cap_c),
        grid_spec=pltpu.PrefetchScalarGridSpec(
            num_scalar_prefetch=1, grid=(b, n_steps),
            in_specs=[tok_spec,
                      pl.BlockSpec((1, ne, slots, d), lambda bi, ti, s: (bi, 0, 0, 0)),
                      pl.BlockSpec((1, 1, ne, TOKEN_BLOCK), lambda bi, ti, s: (bi, ti, 0, 0)),
                      pl.BlockSpec((1, 1, ne, TOKEN_BLOCK), lambda bi, ti, s: (bi, ti, 0, 0)),
                      pl.BlockSpec((1, 1, 6, d), lambda bi, ti, s: (bi, ti // nlb, 0, 0))],
            out_specs=tok_spec,
            scratch_shapes=[pltpu.VMEM((TOKEN_BLOCK, d), F32)]),
        out_shape=jax.ShapeDtypeStruct((b, out_rows, d), F32),
        compiler_params=_cparams(2),
        name="moe_combine",
    )(cnt_flat, xs, ye, pos, afft, modt)


def _rope_tables(n_lat, n_ctx):
    rows = n_lat // GRID_W
    row = jnp.repeat(jnp.arange(rows), GRID_W).astype(F32)
    col = jnp.tile(jnp.arange(GRID_W), rows).astype(F32)
    half = HEAD_DIM // 2
    inv_freq = ROPE_THETA ** (-jnp.arange(0, half, 2, dtype=F32) / half)
    ar = row[:, None] * inv_freq
    ac = col[:, None] * inv_freq
    cos64 = jnp.concatenate([jnp.cos(ar), jnp.cos(ar), jnp.cos(ac), jnp.cos(ac)], axis=1)
    sin64 = jnp.concatenate([-jnp.sin(ar), jnp.sin(ar), -jnp.sin(ac), jnp.sin(ac)], axis=1)
    cos_t = jnp.concatenate([jnp.tile(cos64, (1, LANES // HEAD_DIM)), jnp.ones((n_ctx, LANES), F32)], axis=0)
    sin_t = jnp.concatenate([jnp.tile(sin64, (1, LANES // HEAD_DIM)), jnp.zeros((n_ctx, LANES), F32)], axis=0)
    return cos_t, sin_t


def _swa_head_order():
    heads = []
    for j in range(SWA_KV_HEADS // 2):
        for g in range(SWA_GROUP):
            heads += [2 * SWA_GROUP * j + g, 2 * SWA_GROUP * j + SWA_GROUP + g]
    return np.concatenate([np.arange(h * HEAD_DIM, (h + 1) * HEAD_DIM) for h in heads])


def kernel(x, c, ctx, c_ctx, w_ada, b_ada, norm1_g, w_in, b_gate, diff_q_g, diff_k_g, diff_lambda, diff_subln_g,
           swa_q_g, swa_k_g, swa_sink, w_branch_a, w_branch_b, w_out, norm2_g, w_router, w_e1, w_e3, w_e2):
    b, n_lat, d = x.shape
    n_ctx = ctx.shape[1]
    depth = w_ada.shape[0]
    assert n_lat % TOKEN_BLOCK == 0 and n_ctx % TOKEN_BLOCK == 0 and n_lat >= 2 * TOKEN_BLOCK
    assert d == DIFF_HEADS * 2 * HEAD_DIM == SWA_Q_HEADS * HEAD_DIM and w_router.shape[-1] == N_EXPERTS
    cap_l = CAPACITY_FACTOR * n_lat // N_EXPERTS
    cap_c = CAPACITY_FACTOR * n_ctx // N_EXPERTS
    assert cap_l % SLOT_CHUNK == 0 and cap_c % 16 == 0
    nlb = n_lat // TOKEN_BLOCK

    perm = _swa_head_order()
    kvw = SWA_KV_HEADS * HEAD_DIM
    o = np.cumsum([0, d, d, d, d, kvw, kvw, d, d])
    w_in_p = jnp.concatenate(
        [w_in[..., o[0]:o[3]], w_in[..., o[3]:o[4]][..., perm], w_in[..., o[6]:o[8]], w_in[..., o[4]:o[6]]],
        axis=-1).astype(BF16)
    wa = w_branch_a.astype(BF16)
    wb = w_branch_b[:, perm, :].astype(BF16)
    wo = w_out.astype(BF16)
    wrt =jnp.swapaxes(w_router, 1, 2).astype(BF16)
    w1 = w_e1.astype(BF16)
    w3 = w_e3.astype(BF16)
    w2 = w_e2.astype(BF16)
    scale = HEAD_DIM ** -0.5
    hg = jnp.stack([jnp.tile(diff_q_g, (1, d // HEAD_DIM)) * (scale * math.log2(math.e)),
                    jnp.tile(diff_k_g, (1, d // HEAD_DIM)),
                    jnp.tile(swa_q_g, (1, d // HEAD_DIM)) * (scale * math.log2(math.e)),
                    jnp.tile(swa_k_g, (1, d // HEAD_DIM))], axis=1)
    def score_bound(q_gain, k_gain):
        return (HEAD_DIM * scale * math.log2(math.e)) * jnp.max(jnp.abs(q_gain), axis=1) * jnp.max(jnp.abs(k_gain), axis=1)

    diff_bounded = (score_bound(diff_q_g, diff_k_g) <= SCORE_BOUND_LOG2).astype(I32)
    swa_bounded = ((score_bound(swa_q_g, swa_k_g) <= SCORE_BOUND_LOG2)
                   & (jnp.max(swa_sink, axis=1) * math.log2(math.e) <= SCORE_BOUND_LOG2)).astype(I32)
    subln = jnp.broadcast_to(diff_subln_g[:, :, None], (depth, LANES, LANES))
    g1n = norm1_g.reshape(depth, 1, d)
    g2n = norm2_g.reshape(depth, 1, d)
    bg = b_gate.reshape(depth, 1, 2 * d)
    gm = jnp.asarray(np.kron(np.eye(TOKEN_BLOCK // HEAD_DIM), np.ones((HEAD_DIM, HEAD_DIM))), BF16)
    tri = jnp.asarray(np.triu(np.ones((TOKEN_BLOCK, TOKEN_BLOCK)), 1), BF16)
    cos_t, sin_t = _rope_tables(n_lat, n_ctx)
    key_i = np.arange(2 * TOKEN_BLOCK)[:, None]
    qry_i = np.arange(TOKEN_BLOCK)[None, :]
    band = np.where(np.abs(qry_i + WINDOW - key_i) <= WINDOW, 0.0, NEG_INF)
    band = jnp.asarray(np.concatenate([band, np.zeros((n_ctx, TOKEN_BLOCK))], axis=0), BF16)

    rows = -(-(b + 1) // 8) * 8
    cc = jnp.concatenate([c, c_ctx[None], jnp.zeros((rows - b - 1, d), F32)], axis=0)
    mod_all = _ada_call(cc, w_ada, b_ada)

    xs = jnp.concatenate([x, ctx], axis=1)
    for i in range(depth):
        lam_init = 0.8 - 0.6 * math.exp(-0.3 * i)
        mod_l = mod_all[i, :b].reshape(b, 1, 6, d)
        mod_c = jnp.broadcast_to(mod_all[i, b].reshape(1, 1, 6, d), (b, 1, 6, d))
        modt = jnp.concatenate([mod_l, mod_c], axis=1)
        qa, ka, va, qb, gates, kb, vb = _inproj_call(i, xs, modt, g1n, w_in_p, gm, hg, bg, cos_t, sin_t, n_lat)
        ya = _diff_attn_call(i, diff_bounded[i:i + 1], qa, ka, va, diff_lambda, subln, n_lat, lam_init)
        yb = _swa_call(i, swa_bounded[i:i + 1], swa_sink[i], qb, kb, vb, band, n_lat)
        xs, h2, afft = _merge_call(i, xs, ya, yb, gates, modt, wa, wb, wo, g2n, wrt, n_lat)
        pos, cnt = _route_call(afft, tri, n_lat, cap_l, cap_c)
        cnt_flat = cnt[:, :, :nlb + 1].reshape(-1)
        xe = _gather_call(cnt_flat, h2, pos, n_lat, cap_l, cap_c)
        ye = _ffn_call(i, xe, w1, w3, w2)
        xs = _combine_call(cnt_flat, xs, ye, pos, afft, modt, n_lat, cap_l, cap_c, latent_only=(i == depth - 1))
    return xs
```

```python
import functools
import math

import numpy as np
import jax
import jax.numpy as jnp
from jax import lax
from jax.experimental import pallas as pl
from jax.experimental.pallas import tpu as pltpu

F32 = jnp.float32
BF16 = jnp.bfloat16
I32 = jnp.int32

HEAD_DIM = 64
GRID_W = 64
DIFF_HEADS = 8
SWA_Q_HEADS = 16
SWA_KV_HEADS = 4
SWA_GROUP = SWA_Q_HEADS // SWA_KV_HEADS
WINDOW = 128
N_EXPERTS = 16
CAPACITY_FACTOR = 2
ROPE_THETA = 10000.0
EPS = 1e-6
NEG_INF = -1e30

LANES = 128
TOKEN_BLOCK = 256
SLOT_CHUNK = 128
BF16_ROWS = 16
SCORE_BOUND_LOG2 = 48.0
GATHER_WINDOW = 64
PROJ_CHUNK = 512
FF_CHUNK = 1536
VT_ROWS = LANES + 16
VMEM_LIMIT = 56 * 1024 * 1024


def _cparams(n_axes):
    return pltpu.CompilerParams(dimension_semantics=("arbitrary",) * n_axes, vmem_limit_bytes=VMEM_LIMIT)


def _ada_kernel(c_ref, w_ref, b_ref, o_ref):
    c = c_ref[...]
    sc = c * jax.nn.sigmoid(c)
    o_ref[0] = jnp.dot(sc, w_ref[0], preferred_element_type=F32) + b_ref[0]


def _ada_call(cc, w_ada, b_ada):
    depth, d, six_d = w_ada.shape
    rows = cc.shape[0]
    cols = 1536
    return pl.pallas_call(
        _ada_kernel,
        grid=(depth, six_d // cols),
        in_specs=[pl.BlockSpec((rows, d), lambda i, j: (0, 0)),
                  pl.BlockSpec((1, d, cols), lambda i, j: (i, 0, j)),
                  pl.BlockSpec((1, 1, cols), lambda i, j: (i, 0, j))],
        out_specs=pl.BlockSpec((1, rows, cols), lambda i, j: (i, 0, j)),
        out_shape=jax.ShapeDtypeStruct((depth, rows, six_d), F32),
        compiler_params=_cparams(2),
        name="ada",
    )(cc, w_ada, b_ada.reshape(depth, 1, six_d))


def _inproj_kernel(x_ref, mod_ref, g1_ref, w_ref, gm_ref, hg_ref, bg_ref, cos_ref, sin_ref, *rest, d):
    qa_ref, ka_ref, va_ref, qb_ref, gate_ref, kb_ref, vb_ref = rest[-7:]
    mod = mod_ref[0, 0]
    lane = lax.broadcasted_iota(I32, (1, LANES), 1)
    first_half = (lane % 32) < 16
    gm = gm_ref[...]
    per = PROJ_CHUNK // LANES
    kvw = SWA_KV_HEADS * HEAD_DIM
    ones_rows = jnp.ones((VT_ROWS - LANES, TOKEN_BLOCK), BF16)

    for sub in range(x_ref.shape[1] // TOKEN_BLOCK):
        rows = slice(sub * TOKEN_BLOCK, (sub + 1) * TOKEN_BLOCK)
        x = x_ref[0, rows]
        ms = jnp.mean(x * x, axis=-1, keepdims=True)
        h = x * lax.rsqrt(ms + EPS) * g1_ref[0]
        h = (h * (1.0 + mod[1:2]) + mod[0:1]).astype(BF16)
        cos = cos_ref[rows]
        sin = sin_ref[rows]

        def proj(c0, width, h=h):
            return jnp.dot(h, w_ref[0, :, c0:c0 + width], preferred_element_type=F32)

        def head_norm_rope(p, gain, cos=cos, sin=sin):
            w = p.shape[1]
            gw = gm.shape[0]
            sq = (p * p).astype(BF16)
            msq = jnp.concatenate([jnp.dot(sq[:, g0:g0 + gw], gm, preferred_element_type=F32)
                                   for g0 in range(0, w, gw)], axis=1) * (1.0 / HEAD_DIM)
            qn = p * lax.rsqrt(msq + EPS) * gain
            outs = []
            for u in range(w // LANES):
                seg = qn[:, u * LANES:(u + 1) * LANES]
                partner = jnp.where(first_half, pltpu.roll(seg, LANES - 16, 1), pltpu.roll(seg, 16, 1))
                outs.append((seg * cos + partner * sin).astype(BF16))
            return outs

        for sec, (out_ref, gain_row) in enumerate(((qa_ref, 0), (ka_ref, 1))):
            for c in range(d // PROJ_CHUNK):
                cols = slice(c * PROJ_CHUNK, (c + 1) * PROJ_CHUNK)
                outs = head_norm_rope(proj(sec * d + c * PROJ_CHUNK, PROJ_CHUNK), hg_ref[0, gain_row:gain_row + 1, cols])
                for u, o in enumerate(outs):
                    out_ref[0, c * per + u, rows] = o
        for c in range(d // PROJ_CHUNK):
            p = proj(2 * d + c * PROJ_CHUNK, PROJ_CHUNK)
            for u in range(per):
                va_ref[0, c * per + u, sub, :LANES, :] = p[:, u * LANES:(u + 1) * LANES].T.astype(BF16)
                va_ref[0, c * per + u, sub, LANES:, :] = ones_rows
        for c in range(d // PROJ_CHUNK):
            cols = slice(c * PROJ_CHUNK, (c + 1) * PROJ_CHUNK)
            outs = head_norm_rope(proj(3 * d + c * PROJ_CHUNK, PROJ_CHUNK), hg_ref[0, 2:3, cols])
            for u, o in enumerate(outs):
                qb_ref[0, c * per + u, rows] = o
        outs = head_norm_rope(proj(6 * d, kvw), hg_ref[0, 3:4, :kvw])
        for u, o in enumerate(outs):
            kb_ref[0, u, rows] = o
        p = proj(6 * d + kvw, kvw)
        for u in range(kvw // LANES):
            vb_ref[0, u, sub, :LANES, :] = p[:, u * LANES:(u + 1) * LANES].T.astype(BF16)
            vb_ref[0, u, sub, LANES:, :] = ones_rows
        for c in range(2 * d // PROJ_CHUNK):
            cols = slice(c * PROJ_CHUNK, (c + 1) * PROJ_CHUNK)
            p = proj(4 * d + c * PROJ_CHUNK, PROJ_CHUNK) + bg_ref[0, :, cols]
            gate_ref[0, rows, cols] = jax.nn.sigmoid(p).astype(BF16)


def _inproj_call(layer, xs, modt, norm1_g, w_in_p, gm, hg, bg, cos_t, sin_t, n_lat):
    b, t, d = xs.shape
    nt = t // TOKEN_BLOCK
    n_ctx = t - n_lat
    assert n_lat % n_ctx == 0
    in_w = w_in_p.shape[-1]
    nh = d // LANES
    nkv = SWA_KV_HEADS * HEAD_DIM // LANES
    out_shape = [jax.ShapeDtypeStruct((b, nh, t, LANES), BF16), jax.ShapeDtypeStruct((b, nh, t, LANES), BF16),
                 jax.ShapeDtypeStruct((b, nh, nt, VT_ROWS, TOKEN_BLOCK), BF16),
                 jax.ShapeDtypeStruct((b, nh, t, LANES), BF16), jax.ShapeDtypeStruct((b, t, 2 * d), BF16),
                 jax.ShapeDtypeStruct((b, nkv, t, LANES), BF16),
                 jax.ShapeDtypeStruct((b, nkv, nt, VT_ROWS, TOKEN_BLOCK), BF16)]

    def call(rows, first_block, n_steps, stream, prev):
        nsub = rows // TOKEN_BLOCK

        def tok(bi, ti):
            return (bi, first_block + ti, 0)

        head_spec = pl.BlockSpec((1, nh, rows, LANES), lambda bi, ti: (bi, 0, first_block + ti, 0))
        kv_spec = pl.BlockSpec((1, nkv, rows, LANES), lambda bi, ti: (bi, 0, first_block + ti, 0))
        vt_spec = pl.BlockSpec((1, nh, nsub, VT_ROWS, TOKEN_BLOCK), lambda bi, ti: (bi, 0, first_block + ti, 0, 0))
        kvt_spec = pl.BlockSpec((1, nkv, nsub, VT_ROWS, TOKEN_BLOCK), lambda bi, ti: (bi, 0, first_block + ti, 0, 0))
        operands = (xs, modt, norm1_g, w_in_p, gm, hg, bg, cos_t, sin_t) + tuple(prev)
        return pl.pallas_call(
            functools.partial(_inproj_kernel, d=d),
            grid=(b, n_steps),
            in_specs=[pl.BlockSpec((1, rows, d), tok),
                      pl.BlockSpec((1, 1, 6, d), lambda bi, ti: (bi, stream, 0, 0)),
                      pl.BlockSpec((1, 1, d), lambda bi, ti: (layer, 0, 0)),
                      pl.BlockSpec((1, d, in_w), lambda bi, ti: (layer, 0, 0)),
                      pl.BlockSpec(gm.shape, lambda bi, ti: (0, 0)),
                      pl.BlockSpec((1, 4, d), lambda bi, ti: (layer, 0, 0)),
                      pl.BlockSpec((1, 1, 2 * d), lambda bi, ti: (layer, 0, 0)),
                      pl.BlockSpec((rows, LANES), lambda bi, ti: (first_block + ti, 0)),
                      pl.BlockSpec((rows, LANES), lambda bi, ti: (first_block + ti, 0))]
                     + [pl.BlockSpec(memory_space=pl.ANY)] * len(prev),
            out_specs=[head_spec, head_spec, vt_spec, head_spec, pl.BlockSpec((1, rows, 2 * d), tok),
                       kv_spec, kvt_spec],
            out_shape=out_shape,
            input_output_aliases={9 + n: n for n in range(len(prev))},
            compiler_params=_cparams(2),
            name="inproj",
        )(*operands)

    rows = next(u for u in (2 * TOKEN_BLOCK, TOKEN_BLOCK) if n_lat % u == 0)
    outs = call(rows, 0, n_lat // rows, 0, ())
    return call(n_ctx, n_lat // n_ctx, 1, 1, outs)


def _diff_attn_kernel(flag_ref, q_ref, k_ref, v_ref, dl_ref, sg_ref, *rest, key_blocks, bpu, lam_init):
    o_ref = rest[-1]
    nq = 2 * q_ref.shape[2]
    nn_t = (((1,), (1,)), ((), ()))

    def stacked_queries(hh):
        q = q_ref[0, hh]
        lane = lax.broadcasted_iota(I32, q.shape, 1)
        zero = jnp.zeros_like(q)
        return jnp.concatenate([jnp.where(lane < HEAD_DIM, q, zero), jnp.where(lane >= HEAD_DIM, q, zero)], axis=0)

    def scores(hq, blk):
        hh, q2 = hq
        k = k_ref[0, hh, blk * TOKEN_BLOCK:(blk + 1) * TOKEN_BLOCK, :]
        return lax.dot_general(k, q2, nn_t, preferred_element_type=F32).astype(BF16)

    def process(hq, blocks, s_blocks, state, next_blocks):
        m, l, acc = state
        m_blk = s_blocks[0]
        for s in s_blocks[1:]:
            m_blk = jnp.maximum(m_blk, s)
        m_new = jnp.maximum(m, jnp.max(m_blk, axis=0, keepdims=True).astype(F32))
        alpha = jnp.exp2(m - m_new)
        m_bf = m_new.astype(BF16)
        s_next = []
        r = None
        for c, (blk, s) in enumerate(zip(blocks, s_blocks)):
            if c < len(next_blocks):
                s_next.append(scores(hq, next_blocks[c]))
            p = jnp.exp2(s - m_bf)
            part = jnp.dot(v_ref[0, hq[0], blk], p, preferred_element_type=F32)
            r = part if r is None else r + part
        s_next += [scores(hq, blk) for blk in next_blocks[len(blocks):]]
        return (m_new, alpha * l + r[LANES:LANES + 1], alpha * acc + r[:LANES]), s_next

    init = (jnp.full((1, nq), NEG_INF, F32), jnp.zeros((1, nq), F32), jnp.zeros((LANES, nq), F32))

    def finish(hh, state):
        _, l, acc = state
        o = acc / l
        dl = dl_ref[0]
        lam = (jnp.exp(jnp.sum(dl[0:1] * dl[1:2], axis=1, keepdims=True))
               - jnp.exp(jnp.sum(dl[2:3] * dl[3:4], axis=1, keepdims=True)) + lam_init)
        y = o[:, :nq // 2] - lam * o[:, nq // 2:]
        y = y * lax.rsqrt(jnp.mean(y * y, axis=0, keepdims=True) + EPS) * (sg_ref[0][:, 0:1] * (1.0 - lam_init))
        o_ref[0, hh] = y.astype(BF16)

    def bounded(hh):
        hq = (hh, stacked_queries(hh))
        r = None
        s = scores(hq, key_blocks[0])
        for n, blk in enumerate(key_blocks):
            s_next = scores(hq, key_blocks[n + 1]) if n + 1 < len(key_blocks) else None
            part = jnp.dot(v_ref[0, hh, blk], jnp.exp2(s), preferred_element_type=F32)
            r = part if r is None else r + part
            s = s_next
        finish(hh, (None, r[LANES:LANES + 1], r[:LANES]))

    def online(hh):
        hq = (hh, stacked_queries(hh))
        units = [key_blocks[u:u + bpu] for u in range(0, len(key_blocks), bpu)]
        state = init
        s_cur = [scores(hq, blk) for blk in units[0]]
        for u, blocks in enumerate(units):
            state, s_cur = process(hq, blocks, s_cur, state, units[u + 1] if u + 1 < len(units) else [])
        finish(hh, state)

    @pl.when(flag_ref[0] > 0)
    def _():
        for hh in range(q_ref.shape[1]):
            bounded(hh)

    @pl.when(flag_ref[0] <= 0)
    def _():
        for hh in range(q_ref.shape[1]):
            online(hh)


def _diff_attn_call(layer, flag, qa, ka, vat, diff_lambda, subln_g, n_lat, lam_init):
    b, nh, t, _ = qa.shape
    nt = t // TOKEN_BLOCK
    nlb = n_lat // TOKEN_BLOCK
    n_ctx = t - n_lat
    assert n_lat % n_ctx == 0
    tq = next(u for u in (2 * TOKEN_BLOCK, TOKEN_BLOCK) if n_lat % u == 0)
    hps = 2 if nh % 2 == 0 else 1
    k_spec = pl.BlockSpec((1, hps, t, LANES), lambda bi, hi, ti, f: (bi, hi, 0, 0))
    v_spec = pl.BlockSpec((1, hps, nt, VT_ROWS, TOKEN_BLOCK), lambda bi, hi, ti, f: (bi, hi, 0, 0, 0))
    par_specs = [pl.BlockSpec((1, 4, HEAD_DIM), lambda bi, hi, ti, f: (layer, 0, 0)),
                 pl.BlockSpec((1, LANES, LANES), lambda bi, hi, ti, f: (layer, 0, 0))]

    def call(rows, first_block, n_steps, key_blocks, prev):
        q_spec = pl.BlockSpec((1, hps, rows, LANES), lambda bi, hi, ti, f: (bi, hi, first_block + ti, 0))
        o_spec = pl.BlockSpec((1, hps, LANES, rows), lambda bi, hi, ti, f: (bi, hi, 0, first_block + ti))
        extra = [] if prev is None else [pl.BlockSpec(memory_space=pl.ANY)]
        operands = (flag, qa, ka, vat, diff_lambda, subln_g) + (() if prev is None else (prev,))
        return pl.pallas_call(
            functools.partial(_diff_attn_kernel, key_blocks=key_blocks, bpu=4, lam_init=lam_init),
            grid_spec=pltpu.PrefetchScalarGridSpec(
                num_scalar_prefetch=1, grid=(b, nh // hps, n_steps),
                in_specs=[q_spec, k_spec, v_spec] + par_specs + extra, out_specs=o_spec),
            out_shape=jax.ShapeDtypeStruct((b, nh, LANES, t), BF16),
            input_output_aliases={} if prev is None else {len(operands) - 1: 0},
            compiler_params=_cparams(3),
            name="diff_attn",
        )(*operands)

    ya = call(tq, 0, n_lat // tq, list(range(nt)), None)
    return call(n_ctx, n_lat // n_ctx, 1, list(range(nlb, nt)), ya)


def _swa_kernel(flag_ref, sink_ref, q_ref, k_ref, v_ref, band_ref, o_ref, *, nlb):
    i = pl.program_id(1)
    npair = k_ref.shape[1]
    nt = v_ref.shape[2]
    n_lat = nlb * TOKEN_BLOCK
    is_lat = i < nlb
    im1 = jnp.maximum(i - 1, 0)
    ip1 = jnp.minimum(i + 1, nt - 1)
    nk = 2 * TOKEN_BLOCK + (nt - nlb) * TOKEN_BLOCK

    n_chunks = nk // TOKEN_BLOCK

    def key_chunk(j, c):
        def krows(blk, lo):
            return k_ref[0, j, pl.ds(pl.multiple_of(blk * TOKEN_BLOCK + lo, WINDOW), WINDOW), :]
        if c == 0:
            return jnp.concatenate([krows(im1, WINDOW), krows(i, 0)], axis=0)
        if c == 1:
            return jnp.concatenate([krows(i, WINDOW), krows(ip1, 0)], axis=0)
        return k_ref[0, j, (nlb + c - 2) * TOKEN_BLOCK:(nlb + c - 1) * TOKEN_BLOCK, :]

    def value_chunk(j, half, c):
        if c == 0:
            v = jnp.concatenate([v_ref[0, j, im1][:, WINDOW:], v_ref[0, j, i][:, :WINDOW]], axis=1)
        elif c == 1:
            v = jnp.concatenate([v_ref[0, j, i][:, WINDOW:], v_ref[0, j, ip1][:, :WINDOW]], axis=1)
        else:
            v = v_ref[0, j, nlb + c - 2]
        return jnp.concatenate([v[half * HEAD_DIM:(half + 1) * HEAD_DIM], v[LANES:]], axis=0)
    neg = jnp.float32(NEG_INF)
    zero = jnp.float32(0.0)
    pen = (jnp.where((i == 0) | jnp.logical_not(is_lat), neg, zero), jnp.where(is_lat, zero, neg),
           jnp.where(i >= nlb - 1, neg, zero))
    part = jnp.concatenate([jnp.full((WINDOW, TOKEN_BLOCK), pen[0], F32), jnp.full((TOKEN_BLOCK, TOKEN_BLOCK), pen[1], F32),
                            jnp.full((WINDOW, TOKEN_BLOCK), pen[2], F32),
                            jnp.zeros((nk - 2 * TOKEN_BLOCK, TOKEN_BLOCK), F32)], axis=0)
    bias1 = band_ref[...] + part.astype(BF16)
    bias = jnp.concatenate([bias1] * SWA_GROUP, axis=1)
    lane = lax.broadcasted_iota(I32, (TOKEN_BLOCK, LANES), 1)
    nn_t = (((1,), (1,)), ((), ()))
    log2e = math.log2(math.e)

    kv_heads = [(j, half) for j in range(npair) for half in range(2)]

    def queries(j, half):
        in_half = (lane >= half * HEAD_DIM) & (lane < (half + 1) * HEAD_DIM)
        return jnp.concatenate([jnp.where(in_half, q_ref[0, SWA_GROUP * j + g], jnp.zeros((TOKEN_BLOCK, LANES), BF16))
                                for g in range(SWA_GROUP)], axis=0)

    def scores(j, qs, c):
        s = lax.dot_general(key_chunk(j, c), qs, nn_t, preferred_element_type=F32).astype(BF16)
        return s + bias[c * TOKEN_BLOCK:(c + 1) * TOKEN_BLOCK]

    def sinks(j, half):
        return jnp.concatenate(
            [jnp.full((1, TOKEN_BLOCK), sink_ref[2 * SWA_GROUP * j + SWA_GROUP * half + g] * log2e, F32)
             for g in range(SWA_GROUP)], axis=1)

    def store(j, o0, o1):
        for g in range(SWA_GROUP):
            sl = slice(g * TOKEN_BLOCK, (g + 1) * TOKEN_BLOCK)
            o_ref[0, SWA_GROUP * j + g] = jnp.concatenate([o0[:, sl], o1[:, sl]], axis=0).astype(BF16)

    @pl.when(flag_ref[0] > 0)
    def _():
        steps = [(h, c) for h in range(len(kv_heads)) for c in range(n_chunks)]
        qs = [queries(j, half) for j, half in kv_heads]
        r = [None] * len(kv_heads)
        s = scores(kv_heads[0][0], qs[0], 0)
        for n, (h, c) in enumerate(steps):
            s_next = None
            if n + 1 < len(steps):
                h2, c2 = steps[n + 1]
                s_next = scores(kv_heads[h2][0], qs[h2], c2)
            part = jnp.dot(value_chunk(*kv_heads[h], c), jnp.exp2(s), preferred_element_type=F32)
            r[h] = part if r[h] is None else r[h] + part
            s = s_next
        outs = [r[h][:HEAD_DIM] / (r[h][HEAD_DIM:HEAD_DIM + 1] + jnp.exp2(sinks(*kv_heads[h])))
                for h in range(len(kv_heads))]
        for j in range(npair):
            store(j, outs[2 * j], outs[2 * j + 1])

    @pl.when(flag_ref[0] <= 0)
    def _():
        for j in range(npair):
            outs = []
            for half in range(2):
                qs = queries(j, half)
                s = jnp.concatenate([scores(j, qs, c) for c in range(n_chunks)], axis=0)
                sink = sinks(j, half)
                m = jnp.maximum(jnp.max(s, axis=0, keepdims=True).astype(F32), sink).astype(BF16)
                vh = jnp.concatenate([value_chunk(j, half, c) for c in range(n_chunks)], axis=1)
                r = jnp.dot(vh, jnp.exp2(s - m), preferred_element_type=F32)
                outs.append(r[:HEAD_DIM] / (r[HEAD_DIM:HEAD_DIM + 1] + jnp.exp2(sink - m.astype(F32))))
            store(j, *outs)


def _swa_call(layer, flag, sink, qb, kb, vbt, band, n_lat):
    b, nslab, t, _ = qb.shape
    nt = t // TOKEN_BLOCK
    npair = kb.shape[1]
    assert 2 * WINDOW == TOKEN_BLOCK
    q_spec = pl.BlockSpec((1, nslab, TOKEN_BLOCK, LANES), lambda bi, ti, f, s: (bi, 0, ti, 0))
    k_spec = pl.BlockSpec((1, npair, t, LANES), lambda bi, ti, f, s: (bi, 0, 0, 0))
    v_spec = pl.BlockSpec((1, npair, nt, VT_ROWS, TOKEN_BLOCK), lambda bi, ti, f, s: (bi, 0, 0, 0, 0))
    return pl.pallas_call(
        functools.partial(_swa_kernel, nlb=n_lat // TOKEN_BLOCK),
        grid_spec=pltpu.PrefetchScalarGridSpec(
            num_scalar_prefetch=2, grid=(b, nt),
            in_specs=[q_spec, k_spec, v_spec, pl.BlockSpec(band.shape, lambda bi, ti, f, s: (0, 0))],
            out_specs=pl.BlockSpec((1, nslab, LANES, TOKEN_BLOCK), lambda bi, ti, f, s: (bi, 0, 0, ti))),
        out_shape=jax.ShapeDtypeStruct((b, nslab, LANES, t), BF16),
        compiler_params=_cparams(2),
        name="swa_attn",
    )(flag, sink, qb, kb, vbt, band)


def _merge_kernel(x_ref, ya_ref, yb_ref, gate_ref, mod_ref, wa_ref, wb_ref, wo_ref, g2_ref, wrt_ref, *rest, d):
    xo_ref, h2_ref, afft_ref = rest[-3:]
    nh = ya_ref.shape[1]
    tn = (((0,), (0,)), ((), ()))
    mod = mod_ref[0, 0]
    for sub in range(x_ref.shape[1] // TOKEN_BLOCK):
        rows = slice(sub * TOKEN_BLOCK, (sub + 1) * TOKEN_BLOCK)
        ya = jnp.concatenate([ya_ref[0, h, :, rows] for h in range(nh)], axis=0)
        yb = jnp.concatenate([yb_ref[0, h, :, rows] for h in range(nh)], axis=0)
        za = lax.dot_general(ya, wa_ref[0], tn, preferred_element_type=F32)
        zb = lax.dot_general(yb, wb_ref[0], tn, preferred_element_type=F32)
        gate = gate_ref[0, rows].astype(F32)
        u = gate[:, :d] * za + gate[:, d:] * zb
        z = jnp.dot(u.astype(BF16), wo_ref[0], preferred_element_type=F32)
        xn = x_ref[0, rows] + mod[2:3] * z
        xo_ref[0, rows] = xn
        ms = jnp.mean(xn * xn, axis=-1, keepdims=True)
        h2 = xn * lax.rsqrt(ms + EPS) * g2_ref[0]
        h2 = (h2 * (1.0 + mod[4:5]) + mod[3:4]).astype(BF16)
        h2_ref[0, rows] = h2
        lgt = lax.dot_general(wrt_ref[0], h2, (((1,), (1,)), ((), ())), preferred_element_type=F32)
        et = jnp.exp(lgt - jnp.max(lgt, axis=0, keepdims=True))
        afft_ref[0, sub] = et / jnp.sum(et, axis=0, keepdims=True)


def _merge_call(layer, xs, ya, yb, gates, modt, wa, wb, wo, norm2_g, wrt, n_lat):
    b, t, d = xs.shape
    nt = t // TOKEN_BLOCK
    n_ctx = t - n_lat
    nh = ya.shape[1]
    ne = wrt.shape[1]
    w_spec = pl.BlockSpec((1, d, d), lambda bi, ti: (layer, 0, 0))
    out_shape = [jax.ShapeDtypeStruct((b, t, d), F32), jax.ShapeDtypeStruct((b, t, d), BF16),
                 jax.ShapeDtypeStruct((b, nt, ne, TOKEN_BLOCK), F32)]

    def call(rows, first_block, n_steps, stream, prev):
        nsub = rows // TOKEN_BLOCK
        head_spec = pl.BlockSpec((1, nh, LANES, rows), lambda bi, ti: (bi, 0, 0, first_block + ti))
        tok_spec = pl.BlockSpec((1, rows, d), lambda bi, ti: (bi, first_block + ti, 0))
        operands = (xs, ya, yb, gates, modt, wa, wb, wo, norm2_g, wrt) + tuple(prev)
        return pl.pallas_call(
            functools.partial(_merge_kernel, d=d),
            grid=(b, n_steps),
            in_specs=[tok_spec, head_spec, head_spec,
                      pl.BlockSpec((1, rows, 2 * d), lambda bi, ti: (bi, first_block + ti, 0)),
                      pl.BlockSpec((1, 1, 6, d), lambda bi, ti: (bi, stream, 0, 0)),
                      w_spec, w_spec, w_spec,
                      pl.BlockSpec((1, 1, d), lambda bi, ti: (layer, 0, 0)),
                      pl.BlockSpec((1, ne, d), lambda bi, ti: (layer, 0, 0))]
                     + [pl.BlockSpec(memory_space=pl.ANY)] * len(prev),
            out_specs=[tok_spec, tok_spec,
                       pl.BlockSpec((1, nsub, ne, TOKEN_BLOCK), lambda bi, ti: (bi, first_block + ti, 0, 0))],
            out_shape=out_shape,
            input_output_aliases={10 + n: n for n in range(len(prev))},
            compiler_params=_cparams(2),
            name="merge",
        )(*operands)

    rows = next(u for u in (2 * TOKEN_BLOCK, TOKEN_BLOCK) if n_lat % u == 0)
    outs = call(rows, 0, n_lat // rows, 0, ())
    return call(n_ctx, n_lat // n_ctx, 1, 1, outs)


def _route_kernel(afft_ref, tri_ref, pos_ref, cnt_ref, *, nlb, cap_l, cap_c):
    nt = afft_ref.shape[1]
    ne = afft_ref.shape[2]
    tri = tri_ref[...]

    def bits(blk):
        return lax.bitcast_convert_type(afft_ref[0, blk], I32)

    def count(pred_fn, blocks):
        acc = jnp.zeros((ne, TOKEN_BLOCK), F32)
        for blk in blocks:
            acc = acc + jnp.where(pred_fn(bits(blk)), 1.0, 0.0)
        return jnp.sum(acc, axis=1, keepdims=True)

    def select(blocks, cap, base, with_counts):
        def it(k, thr):
            cand = thr | jnp.left_shift(jnp.int32(1), 30 - k)
            return jnp.where(count(lambda bt: bt >= cand, blocks) >= cap, cand, thr)
        thr = lax.fori_loop(0, 31, it, jnp.zeros((ne, 1), I32))
        ties_wanted = cap - count(lambda bt: bt > thr, blocks)
        seen_eq = jnp.zeros((ne, 1), F32)
        seen_sel = jnp.zeros((ne, 1), F32)
        lane = lax.broadcasted_iota(I32, (ne, LANES), 1)
        cntv = jnp.zeros((ne, LANES), F32)
        for n, blk in enumerate(blocks):
            bt = bits(blk)
            eq = bt == thr
            eqf = jnp.where(eq, 1.0, 0.0)
            rank = jnp.dot(eqf.astype(BF16), tri, preferred_element_type=F32) + seen_eq
            self_ = jnp.where(eq, jnp.where(rank < ties_wanted, 1.0, 0.0), jnp.where(bt > thr, 1.0, 0.0))
            sel = self_ > 0.5
            slot = jnp.dot(self_.astype(BF16), tri, preferred_element_type=F32) + seen_sel
            pos_ref[0, blk] = jnp.where(sel, slot.astype(I32) + base, -1)
            if with_counts:
                cntv = jnp.where(lane == n, seen_sel, cntv)
            seen_eq = seen_eq + jnp.sum(eqf, axis=1, keepdims=True)
            seen_sel = seen_sel + jnp.sum(self_, axis=1, keepdims=True)
        if with_counts:
            cntv = jnp.where(lane == len(blocks), seen_sel, cntv)
            cnt_ref[0] = cntv.astype(I32)

    select(list(range(nlb)), cap_l, 0, True)
    select(list(range(nlb, nt)), cap_c, cap_l, False)


def _route_call(afft, tri, n_lat, cap_l, cap_c):
    b, nt, ne, _ = afft.shape
    return pl.pallas_call(
        functools.partial(_route_kernel, nlb=n_lat // TOKEN_BLOCK, cap_l=cap_l, cap_c=cap_c),
        grid=(b,),
        in_specs=[pl.BlockSpec((1, nt, ne, TOKEN_BLOCK), lambda bi: (bi, 0, 0, 0)),
                  pl.BlockSpec((TOKEN_BLOCK, TOKEN_BLOCK), lambda bi: (0, 0))],
        out_specs=[pl.BlockSpec((1, nt, ne, TOKEN_BLOCK), lambda bi: (bi, 0, 0, 0)),
                   pl.BlockSpec((1, ne, LANES), lambda bi: (bi, 0, 0))],
        out_shape=[jax.ShapeDtypeStruct(afft.shape, I32), jax.ShapeDtypeStruct((b, ne, LANES), I32)],
        compiler_params=_cparams(1),
        name="route",
    )(afft, tri)


def _slot_onehot(pos_row, first_slot, n_slots):
    slots = first_slot + lax.broadcasted_iota(I32, (n_slots, pos_row.shape[1]), 0)
    return jnp.where(pos_row == slots, 1.0, 0.0).astype(BF16)


def _gather_kernel(cnt_ref, h2_ref, pos_ref, xe_ref, *, nlb, cap_l, cap_c):
    bi = pl.program_id(0)
    e = pl.program_id(1)
    ne = pl.num_programs(1)
    cbase = (bi * ne + e) * (nlb + 1)
    win = GATHER_WINDOW
    xe_ref[0, 0, :cap_l, :] = jnp.zeros((cap_l, xe_ref.shape[3]), BF16)

    def slot_range(blk):
        return cnt_ref[cbase + blk], cnt_ref[cbase + blk + 1]

    def window_start(lo):
        return jnp.minimum((lo // BF16_ROWS) * BF16_ROWS, cap_l - win)

    def add_rows(blk, t0, s0, n):
        onehot = _slot_onehot(pos_ref[0, blk, pl.ds(e, 1), :], s0, n)
        xe_ref[0, 0, pl.ds(s0, n), :] += jnp.dot(onehot, h2_ref[0, pl.ds(t0, TOKEN_BLOCK), :],
                                                 preferred_element_type=F32).astype(BF16)

    fits = jnp.bool_(True)
    for blk in range(nlb):
        lo, hi = slot_range(blk)
        fits = fits & (hi <= window_start(lo) + win)

    @pl.when(fits)
    def _():
        for blk in range(nlb):
            s0 = pl.multiple_of(window_start(slot_range(blk)[0]), BF16_ROWS)
            add_rows(blk, blk * TOKEN_BLOCK, s0, win)

    @pl.when(jnp.logical_not(fits))
    def _():
        def body(blk, _):
            lo, hi = slot_range(blk)

            def chunk(c, _):
                add_rows(blk, pl.multiple_of(blk * TOKEN_BLOCK, TOKEN_BLOCK),
                         pl.multiple_of(c * SLOT_CHUNK, SLOT_CHUNK), SLOT_CHUNK)
                return 0

            lax.fori_loop(lo // SLOT_CHUNK, jnp.where(hi > lo, (hi + SLOT_CHUNK - 1) // SLOT_CHUNK, 0), chunk, 0)
            return 0

        lax.fori_loop(0, nlb, body, 0)

    if cap_c:
        ctx_acc = jnp.zeros((cap_c, h2_ref.shape[2]), F32)
        for blk in range(nlb, pos_ref.shape[1]):
            onehot = _slot_onehot(pos_ref[0, blk, pl.ds(e, 1), :], cap_l, cap_c)
            t0 = blk * TOKEN_BLOCK
            ctx_acc = ctx_acc + jnp.dot(onehot, h2_ref[0, t0:t0 + TOKEN_BLOCK, :], preferred_element_type=F32)
        xe_ref[0, 0, cap_l:cap_l + cap_c, :] = ctx_acc.astype(BF16)


def _gather_call(cnt_flat, h2, pos, n_lat, cap_l, cap_c):
    b, t, d = h2.shape
    _, nt, ne, _ = pos.shape
    slots = cap_l + cap_c
    return pl.pallas_call(
        functools.partial(_gather_kernel, nlb=n_lat // TOKEN_BLOCK, cap_l=cap_l, cap_c=cap_c),
        grid_spec=pltpu.PrefetchScalarGridSpec(
            num_scalar_prefetch=1, grid=(b, ne),
            in_specs=[pl.BlockSpec((1, t, d), lambda bi, ei, s: (bi, 0, 0)),
                      pl.BlockSpec((1, nt, ne, TOKEN_BLOCK), lambda bi, ei, s: (bi, 0, 0, 0))],
            out_specs=pl.BlockSpec((1, 1, slots, d), lambda bi, ei, s: (bi, ei, 0, 0))),
        out_shape=jax.ShapeDtypeStruct((b, ne, slots, d), BF16),
        compiler_params=_cparams(2),
        name="moe_gather",
    )(cnt_flat, h2, pos)


def _ffn_kernel(x_ref, w1_ref, w3_ref, w2_ref, o_ref):
    x = x_ref[0, 0]
    ff = w1_ref.shape[-1]
    acc = None
    for f0 in range(0, ff, FF_CHUNK):
        f1 = min(ff, f0 + FF_CHUNK)
        a = jnp.dot(x, w1_ref[0, 0, :, f0:f1], preferred_element_type=F32)
        g = jnp.dot(x, w3_ref[0, 0, :, f0:f1], preferred_element_type=F32)
        hid = (a * jax.nn.sigmoid(a) * g).astype(BF16)
        part = jnp.dot(hid, w2_ref[0, 0, f0:f1, :], preferred_element_type=F32)
        acc = part if acc is None else acc + part
    o_ref[0, 0] = acc.astype(BF16)


def _ffn_call(layer, xe, w1, w3, w2):
    b, ne, slots, d = xe.shape
    ff = w1.shape[-1]
    x_spec = pl.BlockSpec((1, 1, slots, d), lambda ei, bi: (bi, ei, 0, 0))
    return pl.pallas_call(
        _ffn_kernel,
        grid=(ne, b),
        in_specs=[x_spec,
                  pl.BlockSpec((1, 1, d, ff), lambda ei, bi: (layer, ei, 0, 0)),
                  pl.BlockSpec((1, 1, d, ff), lambda ei, bi: (layer, ei, 0, 0)),
                  pl.BlockSpec((1, 1, ff, d), lambda ei, bi: (layer, ei, 0, 0))],
        out_specs=x_spec,
        out_shape=jax.ShapeDtypeStruct(xe.shape, BF16),
        compiler_params=_cparams(2),
        name="moe_ffn",
    )(xe, w1, w3, w2)


def _combine_kernel(cnt_ref, x_ref, ye_ref, pos_ref, afft_ref, mod_ref, xo_ref, acc_ref, *, nlb, cap_l, cap_c):
    bi = pl.program_id(0)
    ti = pl.program_id(1)
    ne = ye_ref.shape[1]
    is_lat = ti < nlb
    tn = (((0,), (0,)), ((), ()))
    win = GATHER_WINDOW

    def slot_range(e):
        cbase = (bi * ne + e) * (nlb + 1) + jnp.minimum(ti, nlb - 1)
        return cnt_ref[cbase], cnt_ref[cbase + 1]

    def window_start(lo):
        return jnp.minimum((lo // BF16_ROWS) * BF16_ROWS, cap_l - win)

    def gated_onehot(e, first_slot, n_slots):
        slots = first_slot + lax.broadcasted_iota(I32, (n_slots, TOKEN_BLOCK), 0)
        return jnp.where(pos_ref[0, 0, e:e + 1, :] == slots, afft_ref[0, 0, e:e + 1, :], 0.0).astype(BF16)

    def residual(y):
        xo_ref[0] = x_ref[0] + mod_ref[0, 0][5:6] * y

    fits = is_lat
    for e in range(ne):
        lo, hi = slot_range(e)
        fits = fits & (hi <= window_start(lo) + win)

    def scatter(pieces):
        w = jnp.concatenate([p[0] for p in pieces], axis=0)
        y = jnp.concatenate([p[1] for p in pieces], axis=0)
        residual(lax.dot_general(w, y, tn, preferred_element_type=F32))

    @pl.when(fits)
    def _():
        pieces = []
        for e in range(ne):
            s0 = pl.multiple_of(window_start(slot_range(e)[0]), BF16_ROWS)
            pieces.append((gated_onehot(e, s0, win), ye_ref[0, e, pl.ds(s0, win), :]))
        scatter(pieces)

    @pl.when(is_lat & jnp.logical_not(fits))
    def _():
        acc_ref[...] = jnp.zeros_like(acc_ref)
        for e in range(ne):
            lo, hi = slot_range(e)

            def body(c, _, e=e):
                s0 = pl.multiple_of(c * SLOT_CHUNK, SLOT_CHUNK)
                acc_ref[...] += lax.dot_general(gated_onehot(e, s0, SLOT_CHUNK),
                                                ye_ref[0, e, pl.ds(s0, SLOT_CHUNK), :], tn,
                                                preferred_element_type=F32)
                return 0

            lax.fori_loop(lo // SLOT_CHUNK, jnp.where(hi > lo, (hi + SLOT_CHUNK - 1) // SLOT_CHUNK, 0), body, 0)
        residual(acc_ref[...])

    if cap_c:
        @pl.when(jnp.logical_not(is_lat))
        def _():
            scatter([(gated_onehot(e, cap_l, cap_c), ye_ref[0, e, cap_l:cap_l + cap_c, :]) for e in range(ne)])


def _combine_call(cnt_flat, xs, ye, pos, afft, modt, n_lat, cap_l, cap_c, latent_only):
    b, t, d = xs.shape
    _, nt, ne, _ = pos.shape
    nlb = n_lat // TOKEN_BLOCK
    slots = cap_l + cap_c
    n_steps, out_rows = (nlb, n_lat) if latent_only else (nt, t)
    tok_spec = pl.BlockSpec((1, TOKEN_BLOCK, d), lambda bi, ti, s: (bi, ti, 0))
    return pl.pallas_call(
        functools.partial(_combine_kernel, nlb=nlb, cap_l=cap_l, cap_c=cap_c),
        grid_spec=pltpu.PrefetchScalarGridSpec(
            num_scalar_prefetch=1, grid=(b, n_steps),
            in_specs=[tok_spec,
                      pl.BlockSpec((1, ne, slots, d), lambda bi, ti, s: (bi, 0, 0, 0)),
                      pl.BlockSpec((1, 1, ne, TOKEN_BLOCK), lambda bi, ti, s: (bi, ti, 0, 0)),
                      pl.BlockSpec((1, 1, ne, TOKEN_BLOCK), lambda bi, ti, s: (bi, ti, 0, 0)),
                      pl.BlockSpec((1, 1, 6, d), lambda bi, ti, s: (bi, ti // nlb, 0, 0))],
            out_specs=tok_spec,
            scratch_shapes=[pltpu.VMEM((TOKEN_BLOCK, d), F32)]),
        out_shape=jax.ShapeDtypeStruct((b, out_rows, d), F32),
        compiler_params=_cparams(2),
        name="moe_combine",
    )(cnt_flat, xs, ye, pos, afft, modt)


def _rope_tables(n_lat, n_ctx):
    rows = n_lat // GRID_W
    row = jnp.repeat(jnp.arange(rows), GRID_W).astype(F32)
    col = jnp.tile(jnp.arange(GRID_W), rows).astype(F32)
    half = HEAD_DIM // 2
    inv_freq = ROPE_THETA ** (-jnp.arange(0, half, 2, dtype=F32) / half)
    ar = row[:, None] * inv_freq
    ac = col[:, None] * inv_freq
    cos64 = jnp.concatenate([jnp.cos(ar), jnp.cos(ar), jnp.cos(ac), jnp.cos(ac)], axis=1)
    sin64 = jnp.concatenate([-jnp.sin(ar), jnp.sin(ar), -jnp.sin(ac), jnp.sin(ac)], axis=1)
    cos_t = jnp.concatenate([jnp.tile(cos64, (1, LANES // HEAD_DIM)), jnp.ones((n_ctx, LANES), F32)], axis=0)
    sin_t = jnp.concatenate([jnp.tile(sin64, (1, LANES // HEAD_DIM)), jnp.zeros((n_ctx, LANES), F32)], axis=0)
    return cos_t, sin_t


def _swa_head_order():
    heads = []
    for j in range(SWA_KV_HEADS // 2):
        for g in range(SWA_GROUP):
            heads += [2 * SWA_GROUP * j + g, 2 * SWA_GROUP * j + SWA_GROUP + g]
    return np.concatenate([np.arange(h * HEAD_DIM, (h + 1) * HEAD_DIM) for h in heads])


def kernel(x, c, ctx, c_ctx, w_ada, b_ada, norm1_g, w_in, b_gate, diff_q_g, diff_k_g, diff_lambda, diff_subln_g,
           swa_q_g, swa_k_g, swa_sink, w_branch_a, w_branch_b, w_out, norm2_g, w_router, w_e1, w_e3, w_e2):
    b, n_lat, d = x.shape
    n_ctx = ctx.shape[1]
    depth = w_ada.shape[0]
    assert n_lat % TOKEN_BLOCK == 0 and n_ctx % TOKEN_BLOCK == 0 and n_lat >= 2 * TOKEN_BLOCK
    assert d == DIFF_HEADS * 2 * HEAD_DIM == SWA_Q_HEADS * HEAD_DIM and w_router.shape[-1] == N_EXPERTS
    cap_l = CAPACITY_FACTOR * n_lat // N_EXPERTS
    cap_c = CAPACITY_FACTOR * n_ctx // N_EXPERTS
    assert cap_l % SLOT_CHUNK == 0 and cap_c % 16 == 0
    nlb = n_lat // TOKEN_BLOCK

    perm = _swa_head_order()
    kvw = SWA_KV_HEADS * HEAD_DIM
    o = np.cumsum([0, d, d, d, d, kvw, kvw, d, d])
    w_in_p = jnp.concatenate(
        [w_in[..., o[0]:o[3]], w_in[..., o[3]:o[4]][..., perm], w_in[..., o[6]:o[8]], w_in[..., o[4]:o[6]]],
        axis=-1).astype(BF16)
    wa = w_branch_a.astype(BF16)
    wb = w_branch_b[:, perm, :].astype(BF16)
    wo = w_out.astype(BF16)
    wrt =jnp.swapaxes(w_router, 1, 2).astype(BF16)
    w1 = w_e1.astype(BF16)
    w3 = w_e3.astype(BF16)
    w2 = w_e2.astype(BF16)
    scale = HEAD_DIM ** -0.5
    hg = jnp.stack([jnp.tile(diff_q_g, (1, d // HEAD_DIM)) * (scale * math.log2(math.e)),
                    jnp.tile(diff_k_g, (1, d // HEAD_DIM)),
                    jnp.tile(swa_q_g, (1, d // HEAD_DIM)) * (scale * math.log2(math.e)),
                    jnp.tile(swa_k_g, (1, d // HEAD_DIM))], axis=1)
    def score_bound(q_gain, k_gain):
        return (HEAD_DIM * scale * math.log2(math.e)) * jnp.max(jnp.abs(q_gain), axis=1) * jnp.max(jnp.abs(k_gain), axis=1)

    diff_bounded = (score_bound(diff_q_g, diff_k_g) <= SCORE_BOUND_LOG2).astype(I32)
    swa_bounded = ((score_bound(swa_q_g, swa_k_g) <= SCORE_BOUND_LOG2)
                   & (jnp.max(swa_sink, axis=1) * math.log2(math.e) <= SCORE_BOUND_LOG2)).astype(I32)
    subln = jnp.broadcast_to(diff_subln_g[:, :, None], (depth, LANES, LANES))
    g1n = norm1_g.reshape(depth, 1, d)
    g2n = norm2_g.reshape(depth, 1, d)
    bg = b_gate.reshape(depth, 1, 2 * d)
    gm = jnp.asarray(np.kron(np.eye(TOKEN_BLOCK // HEAD_DIM), np.ones((HEAD_DIM, HEAD_DIM))), BF16)
    tri = jnp.asarray(np.triu(np.ones((TOKEN_BLOCK, TOKEN_BLOCK)), 1), BF16)
    cos_t, sin_t = _rope_tables(n_lat, n_ctx)
    key_i = np.arange(2 * TOKEN_BLOCK)[:, None]
    qry_i = np.arange(TOKEN_BLOCK)[None, :]
    band = np.where(np.abs(qry_i + WINDOW - key_i) <= WINDOW, 0.0, NEG_INF)
    band = jnp.asarray(np.concatenate([band, np.zeros((n_ctx, TOKEN_BLOCK))], axis=0), BF16)

    rows = -(-(b + 1) // 8) * 8
    cc = jnp.concatenate([c, c_ctx[None], jnp.zeros((rows - b - 1, d), F32)], axis=0)
    mod_all = _ada_call(cc, w_ada, b_ada)

    xs = jnp.concatenate([x, ctx], axis=1)
    for i in range(depth):
        lam_init = 0.8 - 0.6 * math.exp(-0.3 * i)
        mod_l = mod_all[i, :b].reshape(b, 1, 6, d)
        mod_c = jnp.broadcast_to(mod_all[i, b].reshape(1, 1, 6, d), (b, 1, 6, d))
        modt = jnp.concatenate([mod_l, mod_c], axis=1)
        qa, ka, va, qb, gates, kb, vb = _inproj_call(i, xs, modt, g1n, w_in_p, gm, hg, bg, cos_t, sin_t, n_lat)
        ya = _diff_attn_call(i, diff_bounded[i:i + 1], qa, ka, va, diff_lambda, subln, n_lat, lam_init)
        yb = _swa_call(i, swa_bounded[i:i + 1], swa_sink[i], qb, kb, vb, band, n_lat)
        xs, h2, afft = _merge_call(i, xs, ya, yb, gates, modt, wa, wb, wo, g2n, wrt, n_lat)
        pos, cnt = _route_call(afft, tri, n_lat, cap_l, cap_c)
        cnt_flat = cnt[:, :, :nlb + 1].reshape(-1)
        last = i == depth - 1
        cap_ctx = 0 if last else cap_c
        xe = _gather_call(cnt_flat, h2, pos, n_lat, cap_l, cap_ctx)
        ye = _ffn_call(i, xe, w1, w3, w2)
        xs = _combine_call(cnt_flat, xs, ye, pos, afft, modt, n_lat, cap_l, cap_ctx, latent_only=last)
    return xs
```

```python
import functools
import math

import numpy as np
import jax
import jax.numpy as jnp
from jax import lax
from jax.experimental import pallas as pl
from jax.experimental.pallas import tpu as pltpu

F32 = jnp.float32
BF16 = jnp.bfloat16
I32 = jnp.int32

HEAD_DIM = 64
GRID_W = 64
DIFF_HEADS = 8
SWA_Q_HEADS = 16
SWA_KV_HEADS = 4
SWA_GROUP = SWA_Q_HEADS // SWA_KV_HEADS
WINDOW = 128
N_EXPERTS = 16
CAPACITY_FACTOR = 2
ROPE_THETA = 10000.0
EPS = 1e-6
NEG_INF = -1e30

LANES = 128
TOKEN_BLOCK = 256
SLOT_CHUNK = 128
BF16_ROWS = 16
SCORE_BOUND_LOG2 = 48.0
GATHER_WINDOW = 64
PROJ_CHUNK = 512
FF_CHUNK = 1536
VT_ROWS = LANES + 16
VMEM_LIMIT = 56 * 1024 * 1024


def _cparams(n_axes):
    return pltpu.CompilerParams(dimension_semantics=("arbitrary",) * n_axes, vmem_limit_bytes=VMEM_LIMIT)


def _ada_kernel(c_ref, w_ref, b_ref, o_ref):
    c = c_ref[...]
    sc = c * jax.nn.sigmoid(c)
    o_ref[0] = jnp.dot(sc, w_ref[0], preferred_element_type=F32) + b_ref[0]


def _ada_call(cc, w_ada, b_ada):
    depth, d, six_d = w_ada.shape
    rows = cc.shape[0]
    cols = 1536
    return pl.pallas_call(
        _ada_kernel,
        grid=(depth, six_d // cols),
        in_specs=[pl.BlockSpec((rows, d), lambda i, j: (0, 0)),
                  pl.BlockSpec((1, d, cols), lambda i, j: (i, 0, j)),
                  pl.BlockSpec((1, 1, cols), lambda i, j: (i, 0, j))],
        out_specs=pl.BlockSpec((1, rows, cols), lambda i, j: (i, 0, j)),
        out_shape=jax.ShapeDtypeStruct((depth, rows, six_d), F32),
        compiler_params=_cparams(2),
        name="ada",
    )(cc, w_ada, b_ada.reshape(depth, 1, six_d))


def _inproj_kernel(x_ref, mod_ref, g1_ref, w_ref, gm_ref, hg_ref, bg_ref, cos_ref, sin_ref, *rest, d):
    qa_ref, ka_ref, va_ref, qb_ref, gate_ref, kb_ref, vb_ref = rest[-7:]
    mod = mod_ref[0, 0]
    lane = lax.broadcasted_iota(I32, (1, LANES), 1)
    first_half = (lane % 32) < 16
    gm = gm_ref[...]
    per = PROJ_CHUNK // LANES
    kvw = SWA_KV_HEADS * HEAD_DIM
    ones_rows = jnp.ones((VT_ROWS - LANES, TOKEN_BLOCK), BF16)

    for sub in range(x_ref.shape[1] // TOKEN_BLOCK):
        rows = slice(sub * TOKEN_BLOCK, (sub + 1) * TOKEN_BLOCK)
        x = x_ref[0, rows]
        ms = jnp.mean(x * x, axis=-1, keepdims=True)
        h = x * lax.rsqrt(ms + EPS) * g1_ref[0]
        h = (h * (1.0 + mod[1:2]) + mod[0:1]).astype(BF16)
        cos = cos_ref[rows]
        sin = sin_ref[rows]

        def proj(c0, width, h=h):
            return jnp.dot(h, w_ref[0, :, c0:c0 + width], preferred_element_type=F32)

        def head_norm_rope(p, gain, cos=cos, sin=sin):
            w = p.shape[1]
            gw = gm.shape[0]
            sq = (p * p).astype(BF16)
            msq = jnp.concatenate([jnp.dot(sq[:, g0:g0 + gw], gm, preferred_element_type=F32)
                                   for g0 in range(0, w, gw)], axis=1) * (1.0 / HEAD_DIM)
            qn = p * lax.rsqrt(msq + EPS) * gain
            outs = []
            for u in range(w // LANES):
                seg = qn[:, u * LANES:(u + 1) * LANES]
                partner = jnp.where(first_half, pltpu.roll(seg, LANES - 16, 1), pltpu.roll(seg, 16, 1))
                outs.append((seg * cos + partner * sin).astype(BF16))
            return outs

        for sec, (out_ref, gain_row) in enumerate(((qa_ref, 0), (ka_ref, 1))):
            for c in range(d // PROJ_CHUNK):
                cols = slice(c * PROJ_CHUNK, (c + 1) * PROJ_CHUNK)
                outs = head_norm_rope(proj(sec * d + c * PROJ_CHUNK, PROJ_CHUNK), hg_ref[0, gain_row:gain_row + 1, cols])
                for u, o in enumerate(outs):
                    out_ref[0, c * per + u, rows] = o
        for c in range(d // PROJ_CHUNK):
            p = proj(2 * d + c * PROJ_CHUNK, PROJ_CHUNK)
            for u in range(per):
                va_ref[0, c * per + u, sub, :LANES, :] = p[:, u * LANES:(u + 1) * LANES].T.astype(BF16)
                va_ref[0, c * per + u, sub, LANES:, :] = ones_rows
        for c in range(d // PROJ_CHUNK):
            cols = slice(c * PROJ_CHUNK, (c + 1) * PROJ_CHUNK)
            outs = head_norm_rope(proj(3 * d + c * PROJ_CHUNK, PROJ_CHUNK), hg_ref[0, 2:3, cols])
            for u, o in enumerate(outs):
                qb_ref[0, c * per + u, rows] = o
        outs = head_norm_rope(proj(6 * d, kvw), hg_ref[0, 3:4, :kvw])
        for u, o in enumerate(outs):
            kb_ref[0, u, rows] = o
        p = proj(6 * d + kvw, kvw)
        for u in range(kvw // LANES):
            vb_ref[0, u, sub, :LANES, :] = p[:, u * LANES:(u + 1) * LANES].T.astype(BF16)
            vb_ref[0, u, sub, LANES:, :] = ones_rows
        for c in range(2 * d // PROJ_CHUNK):
            cols = slice(c * PROJ_CHUNK, (c + 1) * PROJ_CHUNK)
            p = proj(4 * d + c * PROJ_CHUNK, PROJ_CHUNK) + bg_ref[0, :, cols]
            gate_ref[0, rows, cols] = jax.nn.sigmoid(p).astype(BF16)


def _inproj_call(layer, xs, modt, norm1_g, w_in_p, gm, hg, bg, cos_t, sin_t, n_lat):
    b, t, d = xs.shape
    nt = t // TOKEN_BLOCK
    n_ctx = t - n_lat
    assert n_lat % n_ctx == 0
    in_w = w_in_p.shape[-1]
    nh = d // LANES
    nkv = SWA_KV_HEADS * HEAD_DIM // LANES
    out_shape = [jax.ShapeDtypeStruct((b, nh, t, LANES), BF16), jax.ShapeDtypeStruct((b, nh, t, LANES), BF16),
                 jax.ShapeDtypeStruct((b, nh, nt, VT_ROWS, TOKEN_BLOCK), BF16),
                 jax.ShapeDtypeStruct((b, nh, t, LANES), BF16), jax.ShapeDtypeStruct((b, t, 2 * d), BF16),
                 jax.ShapeDtypeStruct((b, nkv, t, LANES), BF16),
                 jax.ShapeDtypeStruct((b, nkv, nt, VT_ROWS, TOKEN_BLOCK), BF16)]

    def call(rows, first_block, n_steps, stream, prev):
        nsub = rows // TOKEN_BLOCK

        def tok(bi, ti):
            return (bi, first_block + ti, 0)

        head_spec = pl.BlockSpec((1, nh, rows, LANES), lambda bi, ti: (bi, 0, first_block + ti, 0))
        kv_spec = pl.BlockSpec((1, nkv, rows, LANES), lambda bi, ti: (bi, 0, first_block + ti, 0))
        vt_spec = pl.BlockSpec((1, nh, nsub, VT_ROWS, TOKEN_BLOCK), lambda bi, ti: (bi, 0, first_block + ti, 0, 0))
        kvt_spec = pl.BlockSpec((1, nkv, nsub, VT_ROWS, TOKEN_BLOCK), lambda bi, ti: (bi, 0, first_block + ti, 0, 0))
        operands = (xs, modt, norm1_g, w_in_p, gm, hg, bg, cos_t, sin_t) + tuple(prev)
        return pl.pallas_call(
            functools.partial(_inproj_kernel, d=d),
            grid=(b, n_steps),
            in_specs=[pl.BlockSpec((1, rows, d), tok),
                      pl.BlockSpec((1, 1, 6, d), lambda bi, ti: (bi, stream, 0, 0)),
                      pl.BlockSpec((1, 1, d), lambda bi, ti: (layer, 0, 0)),
                      pl.BlockSpec((1, d, in_w), lambda bi, ti: (layer, 0, 0)),
                      pl.BlockSpec(gm.shape, lambda bi, ti: (0, 0)),
                      pl.BlockSpec((1, 4, d), lambda bi, ti: (layer, 0, 0)),
                      pl.BlockSpec((1, 1, 2 * d), lambda bi, ti: (layer, 0, 0)),
                      pl.BlockSpec((rows, LANES), lambda bi, ti: (first_block + ti, 0)),
                      pl.BlockSpec((rows, LANES), lambda bi, ti: (first_block + ti, 0))]
                     + [pl.BlockSpec(memory_space=pl.ANY)] * len(prev),
            out_specs=[head_spec, head_spec, vt_spec, head_spec, pl.BlockSpec((1, rows, 2 * d), tok),
                       kv_spec, kvt_spec],
            out_shape=out_shape,
            input_output_aliases={9 + n: n for n in range(len(prev))},
            compiler_params=_cparams(2),
            name="inproj",
        )(*operands)

    rows = next(u for u in (2 * TOKEN_BLOCK, TOKEN_BLOCK) if n_lat % u == 0)
    outs = call(rows, 0, n_lat // rows, 0, ())
    return call(n_ctx, n_lat // n_ctx, 1, 1, outs)


def _diff_attn_kernel(flag_ref, q_ref, k_ref, v_ref, dl_ref, sg_ref, *rest, key_blocks, bpu, lam_init):
    o_ref = rest[-1]
    nq = 2 * q_ref.shape[2]
    nn_t = (((1,), (1,)), ((), ()))

    def stacked_queries(hh):
        q = q_ref[0, hh]
        lane = lax.broadcasted_iota(I32, q.shape, 1)
        zero = jnp.zeros_like(q)
        return jnp.concatenate([jnp.where(lane < HEAD_DIM, q, zero), jnp.where(lane >= HEAD_DIM, q, zero)], axis=0)

    def scores(hq, blk):
        hh, q2 = hq
        k = k_ref[0, hh, blk * TOKEN_BLOCK:(blk + 1) * TOKEN_BLOCK, :]
        return lax.dot_general(k, q2, nn_t, preferred_element_type=F32).astype(BF16)

    def process(hq, blocks, s_blocks, state, next_blocks):
        m, l, acc = state
        m_blk = s_blocks[0]
        for s in s_blocks[1:]:
            m_blk = jnp.maximum(m_blk, s)
        m_new = jnp.maximum(m, jnp.max(m_blk, axis=0, keepdims=True).astype(F32))
        alpha = jnp.exp2(m - m_new)
        m_bf = m_new.astype(BF16)
        s_next = []
        r = None
        for c, (blk, s) in enumerate(zip(blocks, s_blocks)):
            if c < len(next_blocks):
                s_next.append(scores(hq, next_blocks[c]))
            p = jnp.exp2(s - m_bf)
            part = jnp.dot(v_ref[0, hq[0], blk], p, preferred_element_type=F32)
            r = part if r is None else r + part
        s_next += [scores(hq, blk) for blk in next_blocks[len(blocks):]]
        return (m_new, alpha * l + r[LANES:LANES + 1], alpha * acc + r[:LANES]), s_next

    init = (jnp.full((1, nq), NEG_INF, F32), jnp.zeros((1, nq), F32), jnp.zeros((LANES, nq), F32))

    def finish(hh, state):
        _, l, acc = state
        o = acc / l
        dl = dl_ref[0]
        lam = (jnp.exp(jnp.sum(dl[0:1] * dl[1:2], axis=1, keepdims=True))
               - jnp.exp(jnp.sum(dl[2:3] * dl[3:4], axis=1, keepdims=True)) + lam_init)
        y = o[:, :nq // 2] - lam * o[:, nq // 2:]
        y = y * lax.rsqrt(jnp.mean(y * y, axis=0, keepdims=True) + EPS) * (sg_ref[0][:, 0:1] * (1.0 - lam_init))
        o_ref[0, hh] = y.astype(BF16)

    def bounded(hh):
        hq = (hh, stacked_queries(hh))
        r = None
        s = scores(hq, key_blocks[0])
        for n, blk in enumerate(key_blocks):
            s_next = scores(hq, key_blocks[n + 1]) if n + 1 < len(key_blocks) else None
            part = jnp.dot(v_ref[0, hh, blk], jnp.exp2(s), preferred_element_type=F32)
            r = part if r is None else r + part
            s = s_next
        finish(hh, (None, r[LANES:LANES + 1], r[:LANES]))

    def online(hh):
        hq = (hh, stacked_queries(hh))
        units = [key_blocks[u:u + bpu] for u in range(0, len(key_blocks), bpu)]
        state = init
        s_cur = [scores(hq, blk) for blk in units[0]]
        for u, blocks in enumerate(units):
            state, s_cur = process(hq, blocks, s_cur, state, units[u + 1] if u + 1 < len(units) else [])
        finish(hh, state)

    @pl.when(flag_ref[0] > 0)
    def _():
        for hh in range(q_ref.shape[1]):
            bounded(hh)

    @pl.when(flag_ref[0] <= 0)
    def _():
        for hh in range(q_ref.shape[1]):
            online(hh)


def _diff_attn_call(layer, flag, qa, ka, vat, diff_lambda, subln_g, n_lat, lam_init):
    b, nh, t, _ = qa.shape
    nt = t // TOKEN_BLOCK
    nlb = n_lat // TOKEN_BLOCK
    n_ctx = t - n_lat
    assert n_lat % n_ctx == 0
    tq = next(u for u in (2 * TOKEN_BLOCK, TOKEN_BLOCK) if n_lat % u == 0)
    hps = 2 if nh % 2 == 0 else 1
    k_spec = pl.BlockSpec((1, hps, t, LANES), lambda bi, hi, ti, f: (bi, hi, 0, 0))
    v_spec = pl.BlockSpec((1, hps, nt, VT_ROWS, TOKEN_BLOCK), lambda bi, hi, ti, f: (bi, hi, 0, 0, 0))
    par_specs = [pl.BlockSpec((1, 4, HEAD_DIM), lambda bi, hi, ti, f: (layer, 0, 0)),
                 pl.BlockSpec((1, LANES, LANES), lambda bi, hi, ti, f: (layer, 0, 0))]

    def call(rows, first_block, n_steps, key_blocks, prev):
        q_spec = pl.BlockSpec((1, hps, rows, LANES), lambda bi, hi, ti, f: (bi, hi, first_block + ti, 0))
        o_spec = pl.BlockSpec((1, hps, LANES, rows), lambda bi, hi, ti, f: (bi, hi, 0, first_block + ti))
        extra = [] if prev is None else [pl.BlockSpec(memory_space=pl.ANY)]
        operands = (flag, qa, ka, vat, diff_lambda, subln_g) + (() if prev is None else (prev,))
        return pl.pallas_call(
            functools.partial(_diff_attn_kernel, key_blocks=key_blocks, bpu=4, lam_init=lam_init),
            grid_spec=pltpu.PrefetchScalarGridSpec(
                num_scalar_prefetch=1, grid=(b, nh // hps, n_steps),
                in_specs=[q_spec, k_spec, v_spec] + par_specs + extra, out_specs=o_spec),
            out_shape=jax.ShapeDtypeStruct((b, nh, LANES, t), BF16),
            input_output_aliases={} if prev is None else {len(operands) - 1: 0},
            compiler_params=_cparams(3),
            name="diff_attn",
        )(*operands)

    ya = call(tq, 0, n_lat // tq, list(range(nt)), None)
    return call(n_ctx, n_lat // n_ctx, 1, list(range(nlb, nt)), ya)


def _swa_kernel(flag_ref, sink_ref, q_ref, k_ref, v_ref, band_ref, o_ref, *, nlb):
    i = pl.program_id(1)
    npair = k_ref.shape[1]
    nt = v_ref.shape[2]
    n_lat = nlb * TOKEN_BLOCK
    is_lat = i < nlb
    im1 = jnp.maximum(i - 1, 0)
    ip1 = jnp.minimum(i + 1, nt - 1)
    nk = 2 * TOKEN_BLOCK + (nt - nlb) * TOKEN_BLOCK

    n_chunks = nk // TOKEN_BLOCK

    def key_chunk(j, c):
        def krows(blk, lo):
            return k_ref[0, j, pl.ds(pl.multiple_of(blk * TOKEN_BLOCK + lo, WINDOW), WINDOW), :]
        if c == 0:
            return jnp.concatenate([krows(im1, WINDOW), krows(i, 0)], axis=0)
        if c == 1:
            return jnp.concatenate([krows(i, WINDOW), krows(ip1, 0)], axis=0)
        return k_ref[0, j, (nlb + c - 2) * TOKEN_BLOCK:(nlb + c - 1) * TOKEN_BLOCK, :]

    def value_chunk(j, half, c):
        if c == 0:
            v = jnp.concatenate([v_ref[0, j, im1][:, WINDOW:], v_ref[0, j, i][:, :WINDOW]], axis=1)
        elif c == 1:
            v = jnp.concatenate([v_ref[0, j, i][:, WINDOW:], v_ref[0, j, ip1][:, :WINDOW]], axis=1)
        else:
            v = v_ref[0, j, nlb + c - 2]
        return jnp.concatenate([v[half * HEAD_DIM:(half + 1) * HEAD_DIM], v[LANES:]], axis=0)
    neg = jnp.float32(NEG_INF)
    zero = jnp.float32(0.0)
    pen = (jnp.where((i == 0) | jnp.logical_not(is_lat), neg, zero), jnp.where(is_lat, zero, neg),
           jnp.where(i >= nlb - 1, neg, zero))
    part = jnp.concatenate([jnp.full((WINDOW, TOKEN_BLOCK), pen[0], F32), jnp.full((TOKEN_BLOCK, TOKEN_BLOCK), pen[1], F32),
                            jnp.full((WINDOW, TOKEN_BLOCK), pen[2], F32),
                            jnp.zeros((nk - 2 * TOKEN_BLOCK, TOKEN_BLOCK), F32)], axis=0)
    bias1 = band_ref[...] + part.astype(BF16)
    bias = jnp.concatenate([bias1] * SWA_GROUP, axis=1)
    lane = lax.broadcasted_iota(I32, (TOKEN_BLOCK, LANES), 1)
    nn_t = (((1,), (1,)), ((), ()))
    log2e = math.log2(math.e)

    kv_heads = [(j, half) for j in range(npair) for half in range(2)]

    def queries(j, half):
        in_half = (lane >= half * HEAD_DIM) & (lane < (half + 1) * HEAD_DIM)
        return jnp.concatenate([jnp.where(in_half, q_ref[0, SWA_GROUP * j + g], jnp.zeros((TOKEN_BLOCK, LANES), BF16))
                                for g in range(SWA_GROUP)], axis=0)

    def scores(j, qs, c):
        s = lax.dot_general(key_chunk(j, c), qs, nn_t, preferred_element_type=F32).astype(BF16)
        return s + bias[c * TOKEN_BLOCK:(c + 1) * TOKEN_BLOCK]

    def sinks(j, half):
        return jnp.concatenate(
            [jnp.full((1, TOKEN_BLOCK), sink_ref[2 * SWA_GROUP * j + SWA_GROUP * half + g] * log2e, F32)
             for g in range(SWA_GROUP)], axis=1)

    def store(j, o0, o1):
        for g in range(SWA_GROUP):
            sl = slice(g * TOKEN_BLOCK, (g + 1) * TOKEN_BLOCK)
            o_ref[0, SWA_GROUP * j + g] = jnp.concatenate([o0[:, sl], o1[:, sl]], axis=0).astype(BF16)

    @pl.when(flag_ref[0] > 0)
    def _():
        steps = [(h, c) for h in range(len(kv_heads)) for c in range(n_chunks)]
        qs = [queries(j, half) for j, half in kv_heads]
        r = [None] * len(kv_heads)
        s = scores(kv_heads[0][0], qs[0], 0)
        for n, (h, c) in enumerate(steps):
            s_next = None
            if n + 1 < len(steps):
                h2, c2 = steps[n + 1]
                s_next = scores(kv_heads[h2][0], qs[h2], c2)
            part = jnp.dot(value_chunk(*kv_heads[h], c), jnp.exp2(s), preferred_element_type=F32)
            r[h] = part if r[h] is None else r[h] + part
            s = s_next
        outs = [r[h][:HEAD_DIM] / (r[h][HEAD_DIM:HEAD_DIM + 1] + jnp.exp2(sinks(*kv_heads[h])))
                for h in range(len(kv_heads))]
        for j in range(npair):
            store(j, outs[2 * j], outs[2 * j + 1])

    @pl.when(flag_ref[0] <= 0)
    def _():
        for j in range(npair):
            outs = []
            for half in range(2):
                qs = queries(j, half)
                s = jnp.concatenate([scores(j, qs, c) for c in range(n_chunks)], axis=0)
                sink = sinks(j, half)
                m = jnp.maximum(jnp.max(s, axis=0, keepdims=True).astype(F32), sink).astype(BF16)
                vh = jnp.concatenate([value_chunk(j, half, c) for c in range(n_chunks)], axis=1)
                r = jnp.dot(vh, jnp.exp2(s - m), preferred_element_type=F32)
                outs.append(r[:HEAD_DIM] / (r[HEAD_DIM:HEAD_DIM + 1] + jnp.exp2(sink - m.astype(F32))))
            store(j, *outs)


def _swa_call(layer, flag, sink, qb, kb, vbt, band, n_lat):
    b, nslab, t, _ = qb.shape
    nt = t // TOKEN_BLOCK
    npair = kb.shape[1]
    assert 2 * WINDOW == TOKEN_BLOCK
    q_spec = pl.BlockSpec((1, nslab, TOKEN_BLOCK, LANES), lambda bi, ti, f, s: (bi, 0, ti, 0))
    k_spec = pl.BlockSpec((1, npair, t, LANES), lambda bi, ti, f, s: (bi, 0, 0, 0))
    v_spec = pl.BlockSpec((1, npair, nt, VT_ROWS, TOKEN_BLOCK), lambda bi, ti, f, s: (bi, 0, 0, 0, 0))
    return pl.pallas_call(
        functools.partial(_swa_kernel, nlb=n_lat // TOKEN_BLOCK),
        grid_spec=pltpu.PrefetchScalarGridSpec(
            num_scalar_prefetch=2, grid=(b, nt),
            in_specs=[q_spec, k_spec, v_spec, pl.BlockSpec(band.shape, lambda bi, ti, f, s: (0, 0))],
            out_specs=pl.BlockSpec((1, nslab, LANES, TOKEN_BLOCK), lambda bi, ti, f, s: (bi, 0, 0, ti))),
        out_shape=jax.ShapeDtypeStruct((b, nslab, LANES, t), BF16),
        compiler_params=_cparams(2),
        name="swa_attn",
    )(flag, sink, qb, kb, vbt, band)


def _merge_kernel(x_ref, ya_ref, yb_ref, gate_ref, mod_ref, wa_ref, wb_ref, wo_ref, g2_ref, wrt_ref, *rest, d):
    xo_ref, h2_ref, afft_ref = rest[-3:]
    nh = ya_ref.shape[1]
    tn = (((0,), (0,)), ((), ()))
    mod = mod_ref[0, 0]
    for sub in range(x_ref.shape[1] // TOKEN_BLOCK):
        rows = slice(sub * TOKEN_BLOCK, (sub + 1) * TOKEN_BLOCK)
        ya = jnp.concatenate([ya_ref[0, h, :, rows] for h in range(nh)], axis=0)
        yb = jnp.concatenate([yb_ref[0, h, :, rows] for h in range(nh)], axis=0)
        za = lax.dot_general(ya, wa_ref[0], tn, preferred_element_type=F32)
        zb = lax.dot_general(yb, wb_ref[0], tn, preferred_element_type=F32)
        gate = gate_ref[0, rows].astype(F32)
        u = gate[:, :d] * za + gate[:, d:] * zb
        z = jnp.dot(u.astype(BF16), wo_ref[0], preferred_element_type=F32)
        xn = x_ref[0, rows] + mod[2:3] * z
        xo_ref[0, rows] = xn
        ms = jnp.mean(xn * xn, axis=-1, keepdims=True)
        h2 = xn * lax.rsqrt(ms + EPS) * g2_ref[0]
        h2 = (h2 * (1.0 + mod[4:5]) + mod[3:4]).astype(BF16)
        h2_ref[0, rows] = h2
        lgt = lax.dot_general(wrt_ref[0], h2, (((1,), (1,)), ((), ())), preferred_element_type=F32)
        et = jnp.exp(lgt - jnp.max(lgt, axis=0, keepdims=True))
        afft_ref[0, sub] = et / jnp.sum(et, axis=0, keepdims=True)


def _merge_call(layer, xs, ya, yb, gates, modt, wa, wb, wo, norm2_g, wrt, n_lat):
    b, t, d = xs.shape
    nt = t // TOKEN_BLOCK
    n_ctx = t - n_lat
    nh = ya.shape[1]
    ne = wrt.shape[1]
    w_spec = pl.BlockSpec((1, d, d), lambda bi, ti: (layer, 0, 0))
    out_shape = [jax.ShapeDtypeStruct((b, t, d), F32), jax.ShapeDtypeStruct((b, t, d), BF16),
                 jax.ShapeDtypeStruct((b, nt, ne, TOKEN_BLOCK), F32)]

    def call(rows, first_block, n_steps, stream, prev):
        nsub = rows // TOKEN_BLOCK
        head_spec = pl.BlockSpec((1, nh, LANES, rows), lambda bi, ti: (bi, 0, 0, first_block + ti))
        tok_spec = pl.BlockSpec((1, rows, d), lambda bi, ti: (bi, first_block + ti, 0))
        operands = (xs, ya, yb, gates, modt, wa, wb, wo, norm2_g, wrt) + tuple(prev)
        return pl.pallas_call(
            functools.partial(_merge_kernel, d=d),
            grid=(b, n_steps),
            in_specs=[tok_spec, head_spec, head_spec,
                      pl.BlockSpec((1, rows, 2 * d), lambda bi, ti: (bi, first_block + ti, 0)),
                      pl.BlockSpec((1, 1, 6, d), lambda bi, ti: (bi, stream, 0, 0)),
                      w_spec, w_spec, w_spec,
                      pl.BlockSpec((1, 1, d), lambda bi, ti: (layer, 0, 0)),
                      pl.BlockSpec((1, ne, d), lambda bi, ti: (layer, 0, 0))]
                     + [pl.BlockSpec(memory_space=pl.ANY)] * len(prev),
            out_specs=[tok_spec, tok_spec,
                       pl.BlockSpec((1, nsub, ne, TOKEN_BLOCK), lambda bi, ti: (bi, first_block + ti, 0, 0))],
            out_shape=out_shape,
            input_output_aliases={10 + n: n for n in range(len(prev))},
            compiler_params=_cparams(2),
            name="merge",
        )(*operands)

    rows = next(u for u in (2 * TOKEN_BLOCK, TOKEN_BLOCK) if n_lat % u == 0)
    outs = call(rows, 0, n_lat // rows, 0, ())
    return call(n_ctx, n_lat // n_ctx, 1, 1, outs)


def _route_kernel(afft_ref, tri_ref, pos_ref, cnt_ref, *, nlb, cap_l, cap_c):
    nt = afft_ref.shape[1]
    ne = afft_ref.shape[2]
    tri = tri_ref[...]

    def bits(blk):
        return lax.bitcast_convert_type(afft_ref[0, blk], I32)

    def count(pred_fn, blocks):
        acc = jnp.zeros((ne, TOKEN_BLOCK), F32)
        for blk in blocks:
            acc = acc + jnp.where(pred_fn(bits(blk)), 1.0, 0.0)
        return jnp.sum(acc, axis=1, keepdims=True)

    def select(blocks, cap, base, with_counts):
        def it(k, thr):
            cand = thr | jnp.left_shift(jnp.int32(1), 30 - k)
            return jnp.where(count(lambda bt: bt >= cand, blocks) >= cap, cand, thr)
        thr = lax.fori_loop(0, 31, it, jnp.zeros((ne, 1), I32))
        ties_wanted = cap - count(lambda bt: bt > thr, blocks)
        seen_eq = jnp.zeros((ne, 1), F32)
        seen_sel = jnp.zeros((ne, 1), F32)
        lane = lax.broadcasted_iota(I32, (ne, LANES), 1)
        cntv = jnp.zeros((ne, LANES), F32)
        for n, blk in enumerate(blocks):
            bt = bits(blk)
            eq = bt == thr
            eqf = jnp.where(eq, 1.0, 0.0)
            rank = jnp.dot(eqf.astype(BF16), tri, preferred_element_type=F32) + seen_eq
            self_ = jnp.where(eq, jnp.where(rank < ties_wanted, 1.0, 0.0), jnp.where(bt > thr, 1.0, 0.0))
            sel = self_ > 0.5
            slot = jnp.dot(self_.astype(BF16), tri, preferred_element_type=F32) + seen_sel
            pos_ref[0, blk] = jnp.where(sel, slot.astype(I32) + base, -1)
            if with_counts:
                cntv = jnp.where(lane == n, seen_sel, cntv)
            seen_eq = seen_eq + jnp.sum(eqf, axis=1, keepdims=True)
            seen_sel = seen_sel + jnp.sum(self_, axis=1, keepdims=True)
        if with_counts:
            cntv = jnp.where(lane == len(blocks), seen_sel, cntv)
            cnt_ref[0] = cntv.astype(I32)

    select(list(range(nlb)), cap_l, 0, True)
    select(list(range(nlb, nt)), cap_c, cap_l, False)


def _route_call(afft, tri, n_lat, cap_l, cap_c):
    b, nt, ne, _ = afft.shape
    return pl.pallas_call(
        functools.partial(_route_kernel, nlb=n_lat // TOKEN_BLOCK, cap_l=cap_l, cap_c=cap_c),
        grid=(b,),
        in_specs=[pl.BlockSpec((1, nt, ne, TOKEN_BLOCK), lambda bi: (bi, 0, 0, 0)),
                  pl.BlockSpec((TOKEN_BLOCK, TOKEN_BLOCK), lambda bi: (0, 0))],
        out_specs=[pl.BlockSpec((1, nt, ne, TOKEN_BLOCK), lambda bi: (bi, 0, 0, 0)),
                   pl.BlockSpec((1, ne, LANES), lambda bi: (bi, 0, 0))],
        out_shape=[jax.ShapeDtypeStruct(afft.shape, I32), jax.ShapeDtypeStruct((b, ne, LANES), I32)],
        compiler_params=_cparams(1),
        name="route",
    )(afft, tri)


def _slot_onehot(pos_row, first_slot, n_slots):
    slots = first_slot + lax.broadcasted_iota(I32, (n_slots, pos_row.shape[1]), 0)
    return jnp.where(pos_row == slots, 1.0, 0.0).astype(BF16)


def _gather_kernel(cnt_ref, h2_ref, pos_ref, xe_ref, *, nlb, cap_l, cap_c):
    for k in range(xe_ref.shape[1]):
        _gather_expert(k, cnt_ref, h2_ref, pos_ref, xe_ref, nlb=nlb, cap_l=cap_l, cap_c=cap_c)


def _gather_expert(k, cnt_ref, h2_ref, pos_ref, xe_ref, *, nlb, cap_l, cap_c):
    bi = pl.program_id(0)
    e = pl.program_id(1) * xe_ref.shape[1] + k
    ne = pl.num_programs(1) * xe_ref.shape[1]
    cbase = (bi * ne + e) * (nlb + 1)
    win = GATHER_WINDOW
    xe_ref[0, k, :cap_l, :] = jnp.zeros((cap_l, xe_ref.shape[3]), BF16)

    def slot_range(blk):
        return cnt_ref[cbase + blk], cnt_ref[cbase + blk + 1]

    def window_start(lo):
        return jnp.minimum((lo // BF16_ROWS) * BF16_ROWS, cap_l - win)

    def add_rows(blk, t0, s0, n):
        onehot = _slot_onehot(pos_ref[0, blk, pl.ds(e, 1), :], s0, n)
        xe_ref[0, k, pl.ds(s0, n), :] += jnp.dot(onehot, h2_ref[0, pl.ds(t0, TOKEN_BLOCK), :],
                                                 preferred_element_type=F32).astype(BF16)

    fits = jnp.bool_(True)
    for blk in range(nlb):
        lo, hi = slot_range(blk)
        fits = fits & (hi <= window_start(lo) + win)

    @pl.when(fits)
    def _():
        for blk in range(nlb):
            s0 = pl.multiple_of(window_start(slot_range(blk)[0]), BF16_ROWS)
            add_rows(blk, blk * TOKEN_BLOCK, s0, win)

    @pl.when(jnp.logical_not(fits))
    def _():
        def body(blk, _):
            lo, hi = slot_range(blk)

            def chunk(c, _):
                add_rows(blk, pl.multiple_of(blk * TOKEN_BLOCK, TOKEN_BLOCK),
                         pl.multiple_of(c * SLOT_CHUNK, SLOT_CHUNK), SLOT_CHUNK)
                return 0

            lax.fori_loop(lo // SLOT_CHUNK, jnp.where(hi > lo, (hi + SLOT_CHUNK - 1) // SLOT_CHUNK, 0), chunk, 0)
            return 0

        lax.fori_loop(0, nlb, body, 0)

    if cap_c:
        ctx_acc = jnp.zeros((cap_c, h2_ref.shape[2]), F32)
        for blk in range(nlb, pos_ref.shape[1]):
            onehot = _slot_onehot(pos_ref[0, blk, pl.ds(e, 1), :], cap_l, cap_c)
            t0 = blk * TOKEN_BLOCK
            ctx_acc = ctx_acc + jnp.dot(onehot, h2_ref[0, t0:t0 + TOKEN_BLOCK, :], preferred_element_type=F32)
        xe_ref[0, k, cap_l:cap_l + cap_c, :] = ctx_acc.astype(BF16)


def _gather_call(cnt_flat, h2, pos, n_lat, cap_l, cap_c):
    b, t, d = h2.shape
    _, nt, ne, _ = pos.shape
    slots = cap_l + cap_c
    eps = 2 if ne % 2 == 0 else 1
    return pl.pallas_call(
        functools.partial(_gather_kernel, nlb=n_lat // TOKEN_BLOCK, cap_l=cap_l, cap_c=cap_c),
        grid_spec=pltpu.PrefetchScalarGridSpec(
            num_scalar_prefetch=1, grid=(b, ne // eps),
            in_specs=[pl.BlockSpec((1, t, d), lambda bi, ei, s: (bi, 0, 0)),
                      pl.BlockSpec((1, nt, ne, TOKEN_BLOCK), lambda bi, ei, s: (bi, 0, 0, 0))],
            out_specs=pl.BlockSpec((1, eps, slots, d), lambda bi, ei, s: (bi, ei, 0, 0))),
        out_shape=jax.ShapeDtypeStruct((b, ne, slots, d), BF16),
        compiler_params=_cparams(2),
        name="moe_gather",
    )(cnt_flat, h2, pos)


def _ffn_kernel(x_ref, w1_ref, w3_ref, w2_ref, o_ref):
    ff = w1_ref.shape[-1]
    for s in range(x_ref.shape[0]):
        x = x_ref[s, 0]
        acc = None
        for f0 in range(0, ff, FF_CHUNK):
            f1 = min(ff, f0 + FF_CHUNK)
            a = jnp.dot(x, w1_ref[0, 0, :, f0:f1], preferred_element_type=F32)
            g = jnp.dot(x, w3_ref[0, 0, :, f0:f1], preferred_element_type=F32)
            hid = (a * jax.nn.sigmoid(a) * g).astype(BF16)
            part = jnp.dot(hid, w2_ref[0, 0, f0:f1, :], preferred_element_type=F32)
            acc = part if acc is None else acc + part
        o_ref[s, 0] = acc.astype(BF16)


def _ffn_call(layer, xe, w1, w3, w2):
    b, ne, slots, d = xe.shape
    ff = w1.shape[-1]
    spb = 2 if b % 2 == 0 else 1
    x_spec = pl.BlockSpec((spb, 1, slots, d), lambda ei, bi: (bi, ei, 0, 0))
    return pl.pallas_call(
        _ffn_kernel,
        grid=(ne, b // spb),
        in_specs=[x_spec,
                  pl.BlockSpec((1, 1, d, ff), lambda ei, bi: (layer, ei, 0, 0)),
                  pl.BlockSpec((1, 1, d, ff), lambda ei, bi: (layer, ei, 0, 0)),
                  pl.BlockSpec((1, 1, ff, d), lambda ei, bi: (layer, ei, 0, 0))],
        out_specs=x_spec,
        out_shape=jax.ShapeDtypeStruct(xe.shape, BF16),
        compiler_params=_cparams(2),
        name="moe_ffn",
    )(xe, w1, w3, w2)


def _combine_kernel(cnt_ref, x_ref, ye_ref, pos_ref, afft_ref, mod_ref, *rest, first_tile, nlb, cap_l, cap_c):
    xo_ref, acc_ref = rest[-2:]
    for sub in range(x_ref.shape[1] // TOKEN_BLOCK):
        _combine_tile(sub, first_tile, cnt_ref, x_ref, ye_ref, pos_ref, afft_ref, mod_ref, xo_ref, acc_ref,
                      nlb=nlb, cap_l=cap_l, cap_c=cap_c)


def _combine_tile(sub, first_tile, cnt_ref, x_ref, ye_ref, pos_ref, afft_ref, mod_ref, xo_ref, acc_ref, *,
                  nlb, cap_l, cap_c):
    bi = pl.program_id(0)
    ti = first_tile + pl.program_id(1) * (x_ref.shape[1] // TOKEN_BLOCK) + sub
    rows = slice(sub * TOKEN_BLOCK, (sub + 1) * TOKEN_BLOCK)
    ne = ye_ref.shape[1]
    is_lat = ti < nlb
    tn = (((0,), (0,)), ((), ()))
    win = GATHER_WINDOW

    def slot_range(e):
        cbase = (bi * ne + e) * (nlb + 1) + jnp.minimum(ti, nlb - 1)
        return cnt_ref[cbase], cnt_ref[cbase + 1]

    def window_start(lo):
        return jnp.minimum((lo // BF16_ROWS) * BF16_ROWS, cap_l - win)

    def gated_onehot(e, first_slot, n_slots):
        slots = first_slot + lax.broadcasted_iota(I32, (n_slots, TOKEN_BLOCK), 0)
        return jnp.where(pos_ref[0, sub, e:e + 1, :] == slots, afft_ref[0, sub, e:e + 1, :], 0.0).astype(BF16)

    def residual(y):
        xo_ref[0, rows] = x_ref[0, rows] + mod_ref[0, 0][5:6] * y

    fits = is_lat
    for e in range(ne):
        lo, hi = slot_range(e)
        fits = fits & (hi <= window_start(lo) + win)

    def scatter(pieces):
        w = jnp.concatenate([p[0] for p in pieces], axis=0)
        y = jnp.concatenate([p[1] for p in pieces], axis=0)
        residual(lax.dot_general(w, y, tn, preferred_element_type=F32))

    @pl.when(fits)
    def _():
        pieces = []
        for e in range(ne):
            s0 = pl.multiple_of(window_start(slot_range(e)[0]), BF16_ROWS)
            pieces.append((gated_onehot(e, s0, win), ye_ref[0, e, pl.ds(s0, win), :]))
        scatter(pieces)

    @pl.when(is_lat & jnp.logical_not(fits))
    def _():
        acc_ref[...] = jnp.zeros_like(acc_ref)
        for e in range(ne):
            lo, hi = slot_range(e)

            def body(c, _, e=e):
                s0 = pl.multiple_of(c * SLOT_CHUNK, SLOT_CHUNK)
                acc_ref[...] += lax.dot_general(gated_onehot(e, s0, SLOT_CHUNK),
                                                ye_ref[0, e, pl.ds(s0, SLOT_CHUNK), :], tn,
                                                preferred_element_type=F32)
                return 0

            lax.fori_loop(lo // SLOT_CHUNK, jnp.where(hi > lo, (hi + SLOT_CHUNK - 1) // SLOT_CHUNK, 0), body, 0)
        residual(acc_ref[...])

    if cap_c:
        @pl.when(jnp.logical_not(is_lat))
        def _():
            scatter([(gated_onehot(e, cap_l, cap_c), ye_ref[0, e, cap_l:cap_l + cap_c, :]) for e in range(ne)])


def _combine_call(cnt_flat, xs, ye, pos, afft, modt, n_lat, cap_l, cap_c, latent_only):
    b, t, d = xs.shape
    _, nt, ne, _ = pos.shape
    nlb = n_lat // TOKEN_BLOCK
    slots = cap_l + cap_c
    n_ctx = t - n_lat
    out_rows = n_lat if latent_only else t

    def call(rows, first_block, n_steps, stream, prev):
        nsub = rows // TOKEN_BLOCK
        tok_spec = pl.BlockSpec((1, rows, d), lambda bi, ti, s: (bi, first_block + ti, 0))
        blk_spec = pl.BlockSpec((1, nsub, ne, TOKEN_BLOCK), lambda bi, ti, s: (bi, first_block + ti, 0, 0))
        operands = (cnt_flat, xs, ye, pos, afft, modt) + tuple(prev)
        return pl.pallas_call(
            functools.partial(_combine_kernel, first_tile=first_block * nsub, nlb=nlb, cap_l=cap_l, cap_c=cap_c),
            grid_spec=pltpu.PrefetchScalarGridSpec(
                num_scalar_prefetch=1, grid=(b, n_steps),
                in_specs=[tok_spec,
                          pl.BlockSpec((1, ne, slots, d), lambda bi, ti, s: (bi, 0, 0, 0)),
                          blk_spec, blk_spec,
                          pl.BlockSpec((1, 1, 6, d), lambda bi, ti, s: (bi, stream, 0, 0))]
                         + [pl.BlockSpec(memory_space=pl.ANY)] * len(prev),
                out_specs=tok_spec,
                scratch_shapes=[pltpu.VMEM((TOKEN_BLOCK, d), F32)]),
            out_shape=jax.ShapeDtypeStruct((b, out_rows, d), F32),
            input_output_aliases={6 + n: n for n in range(len(prev))},
            compiler_params=_cparams(2),
            name="moe_combine",
        )(*operands)

    rows = next(u for u in (2 * TOKEN_BLOCK, TOKEN_BLOCK) if n_lat % u == 0)
    out = call(rows, 0, n_lat // rows, 0, ())
    return out if latent_only else call(n_ctx, n_lat // n_ctx, 1, 1, (out,))


def _rope_tables(n_lat, n_ctx):
    rows = n_lat // GRID_W
    row = jnp.repeat(jnp.arange(rows), GRID_W).astype(F32)
    col = jnp.tile(jnp.arange(GRID_W), rows).astype(F32)
    half = HEAD_DIM // 2
    inv_freq = ROPE_THETA ** (-jnp.arange(0, half, 2, dtype=F32) / half)
    ar = row[:, None] * inv_freq
    ac = col[:, None] * inv_freq
    cos64 = jnp.concatenate([jnp.cos(ar), jnp.cos(ar), jnp.cos(ac), jnp.cos(ac)], axis=1)
    sin64 = jnp.concatenate([-jnp.sin(ar), jnp.sin(ar), -jnp.sin(ac), jnp.sin(ac)], axis=1)
    cos_t = jnp.concatenate([jnp.tile(cos64, (1, LANES // HEAD_DIM)), jnp.ones((n_ctx, LANES), F32)], axis=0)
    sin_t = jnp.concatenate([jnp.tile(sin64, (1, LANES // HEAD_DIM)), jnp.zeros((n_ctx, LANES), F32)], axis=0)
    return cos_t, sin_t


def _swa_head_order():
    heads = []
    for j in range(SWA_KV_HEADS // 2):
        for g in range(SWA_GROUP):
            heads += [2 * SWA_GROUP * j + g, 2 * SWA_GROUP * j + SWA_GROUP + g]
    return np.concatenate([np.arange(h * HEAD_DIM, (h + 1) * HEAD_DIM) for h in heads])


def kernel(x, c, ctx, c_ctx, w_ada, b_ada, norm1_g, w_in, b_gate, diff_q_g, diff_k_g, diff_lambda, diff_subln_g,
           swa_q_g, swa_k_g, swa_sink, w_branch_a, w_branch_b, w_out, norm2_g, w_router, w_e1, w_e3, w_e2):
    b, n_lat, d = x.shape
    n_ctx = ctx.shape[1]
    depth = w_ada.shape[0]
    assert n_lat % TOKEN_BLOCK == 0 and n_ctx % TOKEN_BLOCK == 0 and n_lat >= 2 * TOKEN_BLOCK
    assert d == DIFF_HEADS * 2 * HEAD_DIM == SWA_Q_HEADS * HEAD_DIM and w_router.shape[-1] == N_EXPERTS
    cap_l = CAPACITY_FACTOR * n_lat // N_EXPERTS
    cap_c = CAPACITY_FACTOR * n_ctx // N_EXPERTS
    assert cap_l % SLOT_CHUNK == 0 and cap_c % 16 == 0
    nlb = n_lat // TOKEN_BLOCK

    perm = _swa_head_order()
    kvw = SWA_KV_HEADS * HEAD_DIM
    o = np.cumsum([0, d, d, d, d, kvw, kvw, d, d])
    w_in_p = jnp.concatenate(
        [w_in[..., o[0]:o[3]], w_in[..., o[3]:o[4]][..., perm], w_in[..., o[6]:o[8]], w_in[..., o[4]:o[6]]],
        axis=-1).astype(BF16)
    wa = w_branch_a.astype(BF16)
    wb = w_branch_b[:, perm, :].astype(BF16)
    wo = w_out.astype(BF16)
    wrt =jnp.swapaxes(w_router, 1, 2).astype(BF16)
    w1 = w_e1.astype(BF16)
    w3 = w_e3.astype(BF16)
    w2 = w_e2.astype(BF16)
    scale = HEAD_DIM ** -0.5
    hg = jnp.stack([jnp.tile(diff_q_g, (1, d // HEAD_DIM)) * (scale * math.log2(math.e)),
                    jnp.tile(diff_k_g, (1, d // HEAD_DIM)),
                    jnp.tile(swa_q_g, (1, d // HEAD_DIM)) * (scale * math.log2(math.e)),
                    jnp.tile(swa_k_g, (1, d // HEAD_DIM))], axis=1)
    def score_bound(q_gain, k_gain):
        return (HEAD_DIM * scale * math.log2(math.e)) * jnp.max(jnp.abs(q_gain), axis=1) * jnp.max(jnp.abs(k_gain), axis=1)

    diff_bounded = (score_bound(diff_q_g, diff_k_g) <= SCORE_BOUND_LOG2).astype(I32)
    swa_bounded = ((score_bound(swa_q_g, swa_k_g) <= SCORE_BOUND_LOG2)
                   & (jnp.max(swa_sink, axis=1) * math.log2(math.e) <= SCORE_BOUND_LOG2)).astype(I32)
    subln = jnp.broadcast_to(diff_subln_g[:, :, None], (depth, LANES, LANES))
    g1n = norm1_g.reshape(depth, 1, d)
    g2n = norm2_g.reshape(depth, 1, d)
    bg = b_gate.reshape(depth, 1, 2 * d)
    gm = jnp.asarray(np.kron(np.eye(TOKEN_BLOCK // HEAD_DIM), np.ones((HEAD_DIM, HEAD_DIM))), BF16)
    tri = jnp.asarray(np.triu(np.ones((TOKEN_BLOCK, TOKEN_BLOCK)), 1), BF16)
    cos_t, sin_t = _rope_tables(n_lat, n_ctx)
    key_i = np.arange(2 * TOKEN_BLOCK)[:, None]
    qry_i = np.arange(TOKEN_BLOCK)[None, :]
    band = np.where(np.abs(qry_i + WINDOW - key_i) <= WINDOW, 0.0, NEG_INF)
    band = jnp.asarray(np.concatenate([band, np.zeros((n_ctx, TOKEN_BLOCK))], axis=0), BF16)

    rows = -(-(b + 1) // 8) * 8
    cc = jnp.concatenate([c, c_ctx[None], jnp.zeros((rows - b - 1, d), F32)], axis=0)
    mod_all = _ada_call(cc, w_ada, b_ada)

    xs = jnp.concatenate([x, ctx], axis=1)
    for i in range(depth):
        lam_init = 0.8 - 0.6 * math.exp(-0.3 * i)
        mod_l = mod_all[i, :b].reshape(b, 1, 6, d)
        mod_c = jnp.broadcast_to(mod_all[i, b].reshape(1, 1, 6, d), (b, 1, 6, d))
        modt = jnp.concatenate([mod_l, mod_c], axis=1)
        qa, ka, va, qb, gates, kb, vb = _inproj_call(i, xs, modt, g1n, w_in_p, gm, hg, bg, cos_t, sin_t, n_lat)
        ya = _diff_attn_call(i, diff_bounded[i:i + 1], qa, ka, va, diff_lambda, subln, n_lat, lam_init)
        yb = _swa_call(i, swa_bounded[i:i + 1], swa_sink[i], qb, kb, vb, band, n_lat)
        xs, h2, afft = _merge_call(i, xs, ya, yb, gates, modt, wa, wb, wo, g2n, wrt, n_lat)
        pos, cnt = _route_call(afft, tri, n_lat, cap_l, cap_c)
        cnt_flat = cnt[:, :, :nlb + 1].reshape(-1)
        last = i == depth - 1
        cap_ctx = 0 if last else cap_c
        xe = _gather_call(cnt_flat, h2, pos, n_lat, cap_l, cap_ctx)
        ye = _ffn_call(i, xe, w1, w3, w2)
        xs = _combine_call(cnt_flat, xs, ye, pos, afft, modt, n_lat, cap_l, cap_ctx, latent_only=last)
    return xs
```

```python
import functools
import math

import numpy as np
import jax
import jax.numpy as jnp
from jax import lax
from jax.experimental import pallas as pl
from jax.experimental.pallas import tpu as pltpu

F32 = jnp.float32
BF16 = jnp.bfloat16
I32 = jnp.int32

HEAD_DIM = 64
GRID_W = 64
DIFF_HEADS = 8
SWA_Q_HEADS = 16
SWA_KV_HEADS = 4
SWA_GROUP = SWA_Q_HEADS // SWA_KV_HEADS
WINDOW = 128
N_EXPERTS = 16
CAPACITY_FACTOR = 2
ROPE_THETA = 10000.0
EPS = 1e-6
NEG_INF = -1e30

LANES = 128
TOKEN_BLOCK = 256
SLOT_CHUNK = 128
BF16_ROWS = 16
SCORE_BOUND_LOG2 = 48.0
GATHER_WINDOW = 64
PROJ_CHUNK = 512
FF_CHUNK = 1536
VT_ROWS = LANES + 16
VMEM_LIMIT = 56 * 1024 * 1024


def _cparams(n_axes):
    return pltpu.CompilerParams(dimension_semantics=("arbitrary",) * n_axes, vmem_limit_bytes=VMEM_LIMIT)


def _ada_kernel(c_ref, w_ref, b_ref, o_ref):
    c = c_ref[...]
    sc = c * jax.nn.sigmoid(c)
    o_ref[0] = jnp.dot(sc, w_ref[0], preferred_element_type=F32) + b_ref[0]


def _ada_call(cc, w_ada, b_ada):
    depth, d, six_d = w_ada.shape
    rows = cc.shape[0]
    cols = 1536
    return pl.pallas_call(
        _ada_kernel,
        grid=(depth, six_d // cols),
        in_specs=[pl.BlockSpec((rows, d), lambda i, j: (0, 0)),
                  pl.BlockSpec((1, d, cols), lambda i, j: (i, 0, j)),
                  pl.BlockSpec((1, 1, cols), lambda i, j: (i, 0, j))],
        out_specs=pl.BlockSpec((1, rows, cols), lambda i, j: (i, 0, j)),
        out_shape=jax.ShapeDtypeStruct((depth, rows, six_d), F32),
        compiler_params=_cparams(2),
        name="ada",
    )(cc, w_ada, b_ada.reshape(depth, 1, six_d))


def _inproj_kernel(x_ref, mod_ref, g1_ref, w_ref, gm_ref, hg_ref, bg_ref, cos_ref, sin_ref, *rest, d):
    qa_ref, ka_ref, va_ref, qb_ref, gate_ref, kb_ref, vb_ref = rest[-7:]
    mod = mod_ref[0, 0]
    lane = lax.broadcasted_iota(I32, (1, LANES), 1)
    first_half = (lane % 32) < 16
    gm = gm_ref[...]
    per = PROJ_CHUNK // LANES
    kvw = SWA_KV_HEADS * HEAD_DIM
    ones_rows = jnp.ones((VT_ROWS - LANES, TOKEN_BLOCK), BF16)

    for sub in range(x_ref.shape[1] // TOKEN_BLOCK):
        rows = slice(sub * TOKEN_BLOCK, (sub + 1) * TOKEN_BLOCK)
        x = x_ref[0, rows]
        ms = jnp.mean(x * x, axis=-1, keepdims=True)
        h = x * lax.rsqrt(ms + EPS) * g1_ref[0]
        h = (h * (1.0 + mod[1:2]) + mod[0:1]).astype(BF16)
        cos = cos_ref[rows]
        sin = sin_ref[rows]

        def proj(c0, width, h=h):
            return jnp.dot(h, w_ref[0, :, c0:c0 + width], preferred_element_type=F32)

        def head_norm_rope(p, gain, cos=cos, sin=sin):
            w = p.shape[1]
            gw = gm.shape[0]
            sq = (p * p).astype(BF16)
            msq = jnp.concatenate([jnp.dot(sq[:, g0:g0 + gw], gm, preferred_element_type=F32)
                                   for g0 in range(0, w, gw)], axis=1) * (1.0 / HEAD_DIM)
            qn = p * lax.rsqrt(msq + EPS) * gain
            outs = []
            for u in range(w // LANES):
                seg = qn[:, u * LANES:(u + 1) * LANES]
                partner = jnp.where(first_half, pltpu.roll(seg, LANES - 16, 1), pltpu.roll(seg, 16, 1))
                outs.append((seg * cos + partner * sin).astype(BF16))
            return outs

        for sec, (out_ref, gain_row) in enumerate(((qa_ref, 0), (ka_ref, 1))):
            for c in range(d // PROJ_CHUNK):
                cols = slice(c * PROJ_CHUNK, (c + 1) * PROJ_CHUNK)
                outs = head_norm_rope(proj(sec * d + c * PROJ_CHUNK, PROJ_CHUNK), hg_ref[0, gain_row:gain_row + 1, cols])
                for u, o in enumerate(outs):
                    out_ref[0, c * per + u, rows] = o
        for c in range(d // PROJ_CHUNK):
            p = proj(2 * d + c * PROJ_CHUNK, PROJ_CHUNK)
            for u in range(per):
                va_ref[0, c * per + u, sub, :LANES, :] = p[:, u * LANES:(u + 1) * LANES].T.astype(BF16)
                va_ref[0, c * per + u, sub, LANES:, :] = ones_rows
        for c in range(d // PROJ_CHUNK):
            cols = slice(c * PROJ_CHUNK, (c + 1) * PROJ_CHUNK)
            outs = head_norm_rope(proj(3 * d + c * PROJ_CHUNK, PROJ_CHUNK), hg_ref[0, 2:3, cols])
            for u, o in enumerate(outs):
                qb_ref[0, c * per + u, rows] = o
        outs = head_norm_rope(proj(6 * d, kvw), hg_ref[0, 3:4, :kvw])
        for u, o in enumerate(outs):
            kb_ref[0, u, rows] = o
        p = proj(6 * d + kvw, kvw)
        for u in range(kvw // LANES):
            vb_ref[0, u, sub, :LANES, :] = p[:, u * LANES:(u + 1) * LANES].T.astype(BF16)
            vb_ref[0, u, sub, LANES:, :] = ones_rows
        for c in range(2 * d // PROJ_CHUNK):
            cols = slice(c * PROJ_CHUNK, (c + 1) * PROJ_CHUNK)
            p = proj(4 * d + c * PROJ_CHUNK, PROJ_CHUNK) + bg_ref[0, :, cols]
            gate_ref[0, rows, cols] = jax.nn.sigmoid(p).astype(BF16)


def _inproj_call(layer, xs, modt, norm1_g, w_in_p, gm, hg, bg, cos_t, sin_t, n_lat):
    b, t, d = xs.shape
    nt = t // TOKEN_BLOCK
    n_ctx = t - n_lat
    assert n_lat % n_ctx == 0
    in_w = w_in_p.shape[-1]
    nh = d // LANES
    nkv = SWA_KV_HEADS * HEAD_DIM // LANES
    out_shape = [jax.ShapeDtypeStruct((b, nh, t, LANES), BF16), jax.ShapeDtypeStruct((b, nh, t, LANES), BF16),
                 jax.ShapeDtypeStruct((b, nh, nt, VT_ROWS, TOKEN_BLOCK), BF16),
                 jax.ShapeDtypeStruct((b, nh, t, LANES), BF16), jax.ShapeDtypeStruct((b, t, 2 * d), BF16),
                 jax.ShapeDtypeStruct((b, nkv, t, LANES), BF16),
                 jax.ShapeDtypeStruct((b, nkv, nt, VT_ROWS, TOKEN_BLOCK), BF16)]

    def call(rows, first_block, n_steps, stream, prev):
        nsub = rows // TOKEN_BLOCK

        def tok(bi, ti):
            return (bi, first_block + ti, 0)

        head_spec = pl.BlockSpec((1, nh, rows, LANES), lambda bi, ti: (bi, 0, first_block + ti, 0))
        kv_spec = pl.BlockSpec((1, nkv, rows, LANES), lambda bi, ti: (bi, 0, first_block + ti, 0))
        vt_spec = pl.BlockSpec((1, nh, nsub, VT_ROWS, TOKEN_BLOCK), lambda bi, ti: (bi, 0, first_block + ti, 0, 0))
        kvt_spec = pl.BlockSpec((1, nkv, nsub, VT_ROWS, TOKEN_BLOCK), lambda bi, ti: (bi, 0, first_block + ti, 0, 0))
        operands = (xs, modt, norm1_g, w_in_p, gm, hg, bg, cos_t, sin_t) + tuple(prev)
        return pl.pallas_call(
            functools.partial(_inproj_kernel, d=d),
            grid=(b, n_steps),
            in_specs=[pl.BlockSpec((1, rows, d), tok),
                      pl.BlockSpec((1, 1, 6, d), lambda bi, ti: (bi, stream, 0, 0)),
                      pl.BlockSpec((1, 1, d), lambda bi, ti: (layer, 0, 0)),
                      pl.BlockSpec((1, d, in_w), lambda bi, ti: (layer, 0, 0)),
                      pl.BlockSpec(gm.shape, lambda bi, ti: (0, 0)),
                      pl.BlockSpec((1, 4, d), lambda bi, ti: (layer, 0, 0)),
                      pl.BlockSpec((1, 1, 2 * d), lambda bi, ti: (layer, 0, 0)),
                      pl.BlockSpec((rows, LANES), lambda bi, ti: (first_block + ti, 0)),
                      pl.BlockSpec((rows, LANES), lambda bi, ti: (first_block + ti, 0))]
                     + [pl.BlockSpec(memory_space=pl.ANY)] * len(prev),
            out_specs=[head_spec, head_spec, vt_spec, head_spec, pl.BlockSpec((1, rows, 2 * d), tok),
                       kv_spec, kvt_spec],
            out_shape=out_shape,
            input_output_aliases={9 + n: n for n in range(len(prev))},
            compiler_params=_cparams(2),
            name="inproj",
        )(*operands)

    rows = next(u for u in (2 * TOKEN_BLOCK, TOKEN_BLOCK) if n_lat % u == 0)
    outs = call(rows, 0, n_lat // rows, 0, ())
    return call(n_ctx, n_lat // n_ctx, 1, 1, outs)


def _diff_attn_kernel(flag_ref, q_ref, k_ref, v_ref, dl_ref, sg_ref, *rest, key_blocks, bpu, lam_init):
    o_ref = rest[-1]
    nq = 2 * q_ref.shape[2]
    nn_t = (((1,), (1,)), ((), ()))

    def stacked_queries(hh):
        q = q_ref[0, hh]
        lane = lax.broadcasted_iota(I32, q.shape, 1)
        zero = jnp.zeros_like(q)
        return jnp.concatenate([jnp.where(lane < HEAD_DIM, q, zero), jnp.where(lane >= HEAD_DIM, q, zero)], axis=0)

    def scores(hq, blk):
        hh, q2 = hq
        k = k_ref[0, hh, blk * TOKEN_BLOCK:(blk + 1) * TOKEN_BLOCK, :]
        return lax.dot_general(k, q2, nn_t, preferred_element_type=F32).astype(BF16)

    def process(hq, blocks, s_blocks, state, next_blocks):
        m, l, acc = state
        m_blk = s_blocks[0]
        for s in s_blocks[1:]:
            m_blk = jnp.maximum(m_blk, s)
        m_new = jnp.maximum(m, jnp.max(m_blk, axis=0, keepdims=True).astype(F32))
        alpha = jnp.exp2(m - m_new)
        m_bf = m_new.astype(BF16)
        s_next = []
        r = None
        for c, (blk, s) in enumerate(zip(blocks, s_blocks)):
            if c < len(next_blocks):
                s_next.append(scores(hq, next_blocks[c]))
            p = jnp.exp2(s - m_bf)
            part = jnp.dot(v_ref[0, hq[0], blk], p, preferred_element_type=F32)
            r = part if r is None else r + part
        s_next += [scores(hq, blk) for blk in next_blocks[len(blocks):]]
        return (m_new, alpha * l + r[LANES:LANES + 1], alpha * acc + r[:LANES]), s_next

    init = (jnp.full((1, nq), NEG_INF, F32), jnp.zeros((1, nq), F32), jnp.zeros((LANES, nq), F32))

    def finish(hh, state):
        _, l, acc = state
        o = acc / l
        dl = dl_ref[0]
        lam = (jnp.exp(jnp.sum(dl[0:1] * dl[1:2], axis=1, keepdims=True))
               - jnp.exp(jnp.sum(dl[2:3] * dl[3:4], axis=1, keepdims=True)) + lam_init)
        y = o[:, :nq // 2] - lam * o[:, nq // 2:]
        y = y * lax.rsqrt(jnp.mean(y * y, axis=0, keepdims=True) + EPS) * (sg_ref[0][:, 0:1] * (1.0 - lam_init))
        o_ref[0, hh] = y.astype(BF16)

    def bounded(hh):
        hq = (hh, stacked_queries(hh))
        r = None
        s = scores(hq, key_blocks[0])
        for n, blk in enumerate(key_blocks):
            s_next = scores(hq, key_blocks[n + 1]) if n + 1 < len(key_blocks) else None
            part = jnp.dot(v_ref[0, hh, blk], jnp.exp2(s), preferred_element_type=F32)
            r = part if r is None else r + part
            s = s_next
        finish(hh, (None, r[LANES:LANES + 1], r[:LANES]))

    def online(hh):
        hq = (hh, stacked_queries(hh))
        units = [key_blocks[u:u + bpu] for u in range(0, len(key_blocks), bpu)]
        state = init
        s_cur = [scores(hq, blk) for blk in units[0]]
        for u, blocks in enumerate(units):
            state, s_cur = process(hq, blocks, s_cur, state, units[u + 1] if u + 1 < len(units) else [])
        finish(hh, state)

    @pl.when(flag_ref[0] > 0)
    def _():
        for hh in range(q_ref.shape[1]):
            bounded(hh)

    @pl.when(flag_ref[0] <= 0)
    def _():
        for hh in range(q_ref.shape[1]):
            online(hh)


def _diff_attn_call(layer, flag, qa, ka, vat, diff_lambda, subln_g, n_lat, lam_init):
    b, nh, t, _ = qa.shape
    nt = t // TOKEN_BLOCK
    nlb = n_lat // TOKEN_BLOCK
    n_ctx = t - n_lat
    assert n_lat % n_ctx == 0
    tq = next(u for u in (2 * TOKEN_BLOCK, TOKEN_BLOCK) if n_lat % u == 0)
    hps = 2 if nh % 2 == 0 else 1
    par_specs = [pl.BlockSpec((1, 4, HEAD_DIM), lambda bi, hi, ti, f: (layer, 0, 0)),
                 pl.BlockSpec((1, LANES, LANES), lambda bi, hi, ti, f: (layer, 0, 0))]

    def call(rows, first_block, n_steps, key_blocks, prev):
        nkb = len(key_blocks)
        k_spec = pl.BlockSpec((1, hps, nkb * TOKEN_BLOCK, LANES), lambda bi, hi, ti, f: (bi, hi, key_blocks[0] // nkb, 0))
        v_spec = pl.BlockSpec((1, hps, nkb, VT_ROWS, TOKEN_BLOCK),
                              lambda bi, hi, ti, f: (bi, hi, key_blocks[0] // nkb, 0, 0))
        q_spec = pl.BlockSpec((1, hps, rows, LANES), lambda bi, hi, ti, f: (bi, hi, first_block + ti, 0))
        o_spec = pl.BlockSpec((1, hps, LANES, rows), lambda bi, hi, ti, f: (bi, hi, 0, first_block + ti))
        extra = [] if prev is None else [pl.BlockSpec(memory_space=pl.ANY)]
        operands = (flag, qa, ka, vat, diff_lambda, subln_g) + (() if prev is None else (prev,))
        return pl.pallas_call(
            functools.partial(_diff_attn_kernel, key_blocks=list(range(nkb)), bpu=4, lam_init=lam_init),
            grid_spec=pltpu.PrefetchScalarGridSpec(
                num_scalar_prefetch=1, grid=(b, nh // hps, n_steps),
                in_specs=[q_spec, k_spec, v_spec] + par_specs + extra, out_specs=o_spec),
            out_shape=jax.ShapeDtypeStruct((b, nh, LANES, t), BF16),
            input_output_aliases={} if prev is None else {len(operands) - 1: 0},
            compiler_params=_cparams(3),
            name="diff_attn",
        )(*operands)

    ya = call(tq, 0, n_lat // tq, list(range(nt)), None)
    return call(n_ctx, n_lat // n_ctx, 1, list(range(nlb, nt)), ya)


def _swa_kernel(flag_ref, sink_ref, q_ref, k_ref, v_ref, band_ref, o_ref, *, nlb):
    i = pl.program_id(1)
    npair = k_ref.shape[1]
    nt = v_ref.shape[2]
    n_lat = nlb * TOKEN_BLOCK
    is_lat = i < nlb
    im1 = jnp.maximum(i - 1, 0)
    ip1 = jnp.minimum(i + 1, nt - 1)
    nk = 2 * TOKEN_BLOCK + (nt - nlb) * TOKEN_BLOCK

    n_chunks = nk // TOKEN_BLOCK

    def key_chunk(j, c):
        def krows(blk, lo):
            return k_ref[0, j, pl.ds(pl.multiple_of(blk * TOKEN_BLOCK + lo, WINDOW), WINDOW), :]
        if c == 0:
            return jnp.concatenate([krows(im1, WINDOW), krows(i, 0)], axis=0)
        if c == 1:
            return jnp.concatenate([krows(i, WINDOW), krows(ip1, 0)], axis=0)
        return k_ref[0, j, (nlb + c - 2) * TOKEN_BLOCK:(nlb + c - 1) * TOKEN_BLOCK, :]

    def value_chunk(j, half, c):
        if c == 0:
            v = jnp.concatenate([v_ref[0, j, im1][:, WINDOW:], v_ref[0, j, i][:, :WINDOW]], axis=1)
        elif c == 1:
            v = jnp.concatenate([v_ref[0, j, i][:, WINDOW:], v_ref[0, j, ip1][:, :WINDOW]], axis=1)
        else:
            v = v_ref[0, j, nlb + c - 2]
        return jnp.concatenate([v[half * HEAD_DIM:(half + 1) * HEAD_DIM], v[LANES:]], axis=0)
    neg = jnp.float32(NEG_INF)
    zero = jnp.float32(0.0)
    pen = (jnp.where((i == 0) | jnp.logical_not(is_lat), neg, zero), jnp.where(is_lat, zero, neg),
           jnp.where(i >= nlb - 1, neg, zero))
    part = jnp.concatenate([jnp.full((WINDOW, TOKEN_BLOCK), pen[0], F32), jnp.full((TOKEN_BLOCK, TOKEN_BLOCK), pen[1], F32),
                            jnp.full((WINDOW, TOKEN_BLOCK), pen[2], F32),
                            jnp.zeros((nk - 2 * TOKEN_BLOCK, TOKEN_BLOCK), F32)], axis=0)
    bias1 = band_ref[...] + part.astype(BF16)
    bias = jnp.concatenate([bias1] * SWA_GROUP, axis=1)
    lane = lax.broadcasted_iota(I32, (TOKEN_BLOCK, LANES), 1)
    nn_t = (((1,), (1,)), ((), ()))
    log2e = math.log2(math.e)

    kv_heads = [(j, half) for j in range(npair) for half in range(2)]

    def queries(j, half):
        in_half = (lane >= half * HEAD_DIM) & (lane < (half + 1) * HEAD_DIM)
        return jnp.concatenate([jnp.where(in_half, q_ref[0, SWA_GROUP * j + g], jnp.zeros((TOKEN_BLOCK, LANES), BF16))
                                for g in range(SWA_GROUP)], axis=0)

    def scores(j, qs, c):
        s = lax.dot_general(key_chunk(j, c), qs, nn_t, preferred_element_type=F32).astype(BF16)
        return s + bias[c * TOKEN_BLOCK:(c + 1) * TOKEN_BLOCK]

    def sinks(j, half):
        return jnp.concatenate(
            [jnp.full((1, TOKEN_BLOCK), sink_ref[2 * SWA_GROUP * j + SWA_GROUP * half + g] * log2e, F32)
             for g in range(SWA_GROUP)], axis=1)

    def store(j, o0, o1):
        for g in range(SWA_GROUP):
            sl = slice(g * TOKEN_BLOCK, (g + 1) * TOKEN_BLOCK)
            o_ref[0, SWA_GROUP * j + g] = jnp.concatenate([o0[:, sl], o1[:, sl]], axis=0).astype(BF16)

    @pl.when(flag_ref[0] > 0)
    def _():
        steps = [(h, c) for h in range(len(kv_heads)) for c in range(n_chunks)]
        qs = [queries(j, half) for j, half in kv_heads]
        r = [None] * len(kv_heads)
        s = scores(kv_heads[0][0], qs[0], 0)
        for n, (h, c) in enumerate(steps):
            s_next = None
            if n + 1 < len(steps):
                h2, c2 = steps[n + 1]
                s_next = scores(kv_heads[h2][0], qs[h2], c2)
            part = jnp.dot(value_chunk(*kv_heads[h], c), jnp.exp2(s), preferred_element_type=F32)
            r[h] = part if r[h] is None else r[h] + part
            s = s_next
        outs = [r[h][:HEAD_DIM] / (r[h][HEAD_DIM:HEAD_DIM + 1] + jnp.exp2(sinks(*kv_heads[h])))
                for h in range(len(kv_heads))]
        for j in range(npair):
            store(j, outs[2 * j], outs[2 * j + 1])

    @pl.when(flag_ref[0] <= 0)
    def _():
        for j in range(npair):
            outs = []
            for half in range(2):
                qs = queries(j, half)
                s = jnp.concatenate([scores(j, qs, c) for c in range(n_chunks)], axis=0)
                sink = sinks(j, half)
                m = jnp.maximum(jnp.max(s, axis=0, keepdims=True).astype(F32), sink).astype(BF16)
                vh = jnp.concatenate([value_chunk(j, half, c) for c in range(n_chunks)], axis=1)
                r = jnp.dot(vh, jnp.exp2(s - m), preferred_element_type=F32)
                outs.append(r[:HEAD_DIM] / (r[HEAD_DIM:HEAD_DIM + 1] + jnp.exp2(sink - m.astype(F32))))
            store(j, *outs)


def _swa_call(layer, flag, sink, qb, kb, vbt, band, n_lat):
    b, nslab, t, _ = qb.shape
    nt = t // TOKEN_BLOCK
    npair = kb.shape[1]
    assert 2 * WINDOW == TOKEN_BLOCK
    q_spec = pl.BlockSpec((1, nslab, TOKEN_BLOCK, LANES), lambda bi, ti, f, s: (bi, 0, ti, 0))
    k_spec = pl.BlockSpec((1, npair, t, LANES), lambda bi, ti, f, s: (bi, 0, 0, 0))
    v_spec = pl.BlockSpec((1, npair, nt, VT_ROWS, TOKEN_BLOCK), lambda bi, ti, f, s: (bi, 0, 0, 0, 0))
    return pl.pallas_call(
        functools.partial(_swa_kernel, nlb=n_lat // TOKEN_BLOCK),
        grid_spec=pltpu.PrefetchScalarGridSpec(
            num_scalar_prefetch=2, grid=(b, nt),
            in_specs=[q_spec, k_spec, v_spec, pl.BlockSpec(band.shape, lambda bi, ti, f, s: (0, 0))],
            out_specs=pl.BlockSpec((1, nslab, LANES, TOKEN_BLOCK), lambda bi, ti, f, s: (bi, 0, 0, ti))),
        out_shape=jax.ShapeDtypeStruct((b, nslab, LANES, t), BF16),
        compiler_params=_cparams(2),
        name="swa_attn",
    )(flag, sink, qb, kb, vbt, band)


def _merge_kernel(x_ref, ya_ref, yb_ref, gate_ref, mod_ref, wa_ref, wb_ref, wo_ref, g2_ref, wrt_ref, *rest, d):
    xo_ref, h2_ref, afft_ref = rest[-3:]
    nh = ya_ref.shape[1]
    tn = (((0,), (0,)), ((), ()))
    mod = mod_ref[0, 0]
    for sub in range(x_ref.shape[1] // TOKEN_BLOCK):
        rows = slice(sub * TOKEN_BLOCK, (sub + 1) * TOKEN_BLOCK)
        ya = jnp.concatenate([ya_ref[0, h, :, rows] for h in range(nh)], axis=0)
        yb = jnp.concatenate([yb_ref[0, h, :, rows] for h in range(nh)], axis=0)
        za = lax.dot_general(ya, wa_ref[0], tn, preferred_element_type=F32)
        zb = lax.dot_general(yb, wb_ref[0], tn, preferred_element_type=F32)
        gate = gate_ref[0, rows].astype(F32)
        u = gate[:, :d] * za + gate[:, d:] * zb
        z = jnp.dot(u.astype(BF16), wo_ref[0], preferred_element_type=F32)
        xn = x_ref[0, rows] + mod[2:3] * z
        xo_ref[0, rows] = xn
        ms = jnp.mean(xn * xn, axis=-1, keepdims=True)
        h2 = xn * lax.rsqrt(ms + EPS) * g2_ref[0]
        h2 = (h2 * (1.0 + mod[4:5]) + mod[3:4]).astype(BF16)
        h2_ref[0, rows] = h2
        lgt = lax.dot_general(wrt_ref[0], h2, (((1,), (1,)), ((), ())), preferred_element_type=F32)
        et = jnp.exp(lgt - jnp.max(lgt, axis=0, keepdims=True))
        afft_ref[0, sub] = et / jnp.sum(et, axis=0, keepdims=True)


def _merge_call(layer, xs, ya, yb, gates, modt, wa, wb, wo, norm2_g, wrt, n_lat):
    b, t, d = xs.shape
    nt = t // TOKEN_BLOCK
    n_ctx = t - n_lat
    nh = ya.shape[1]
    ne = wrt.shape[1]
    w_spec = pl.BlockSpec((1, d, d), lambda bi, ti: (layer, 0, 0))
    out_shape = [jax.ShapeDtypeStruct((b, t, d), F32), jax.ShapeDtypeStruct((b, t, d), BF16),
                 jax.ShapeDtypeStruct((b, nt, ne, TOKEN_BLOCK), F32)]

    def call(rows, first_block, n_steps, stream, prev):
        nsub = rows // TOKEN_BLOCK
        head_spec = pl.BlockSpec((1, nh, LANES, rows), lambda bi, ti: (bi, 0, 0, first_block + ti))
        tok_spec = pl.BlockSpec((1, rows, d), lambda bi, ti: (bi, first_block + ti, 0))
        operands = (xs, ya, yb, gates, modt, wa, wb, wo, norm2_g, wrt) + tuple(prev)
        return pl.pallas_call(
            functools.partial(_merge_kernel, d=d),
            grid=(b, n_steps),
            in_specs=[tok_spec, head_spec, head_spec,
                      pl.BlockSpec((1, rows, 2 * d), lambda bi, ti: (bi, first_block + ti, 0)),
                      pl.BlockSpec((1, 1, 6, d), lambda bi, ti: (bi, stream, 0, 0)),
                      w_spec, w_spec, w_spec,
                      pl.BlockSpec((1, 1, d), lambda bi, ti: (layer, 0, 0)),
                      pl.BlockSpec((1, ne, d), lambda bi, ti: (layer, 0, 0))]
                     + [pl.BlockSpec(memory_space=pl.ANY)] * len(prev),
            out_specs=[tok_spec, tok_spec,
                       pl.BlockSpec((1, nsub, ne, TOKEN_BLOCK), lambda bi, ti: (bi, first_block + ti, 0, 0))],
            out_shape=out_shape,
            input_output_aliases={10 + n: n for n in range(len(prev))},
            compiler_params=_cparams(2),
            name="merge",
        )(*operands)

    rows = next(u for u in (2 * TOKEN_BLOCK, TOKEN_BLOCK) if n_lat % u == 0)
    outs = call(rows, 0, n_lat // rows, 0, ())
    return call(n_ctx, n_lat // n_ctx, 1, 1, outs)


def _route_kernel(afft_ref, tri_ref, pos_ref, cnt_ref, *, nlb, cap_l, cap_c):
    nt = afft_ref.shape[1]
    ne = afft_ref.shape[2]
    tri = tri_ref[...]

    def bits(blk):
        return lax.bitcast_convert_type(afft_ref[0, blk], I32)

    def count(pred_fn, blocks):
        acc = jnp.zeros((ne, TOKEN_BLOCK), F32)
        for blk in blocks:
            acc = acc + jnp.where(pred_fn(bits(blk)), 1.0, 0.0)
        return jnp.sum(acc, axis=1, keepdims=True)

    def select(blocks, cap, base, with_counts):
        def it(k, thr):
            cand = thr | jnp.left_shift(jnp.int32(1), 30 - k)
            return jnp.where(count(lambda bt: bt >= cand, blocks) >= cap, cand, thr)
        thr = lax.fori_loop(0, 31, it, jnp.zeros((ne, 1), I32))
        ties_wanted = cap - count(lambda bt: bt > thr, blocks)
        seen_eq = jnp.zeros((ne, 1), F32)
        seen_sel = jnp.zeros((ne, 1), F32)
        lane = lax.broadcasted_iota(I32, (ne, LANES), 1)
        cntv = jnp.zeros((ne, LANES), F32)
        for n, blk in enumerate(blocks):
            bt = bits(blk)
            eq = bt == thr
            eqf = jnp.where(eq, 1.0, 0.0)
            rank = jnp.dot(eqf.astype(BF16), tri, preferred_element_type=F32) + seen_eq
            self_ = jnp.where(eq, jnp.where(rank < ties_wanted, 1.0, 0.0), jnp.where(bt > thr, 1.0, 0.0))
            sel = self_ > 0.5
            slot = jnp.dot(self_.astype(BF16), tri, preferred_element_type=F32) + seen_sel
            pos_ref[0, blk] = jnp.where(sel, slot.astype(I32) + base, -1)
            if with_counts:
                cntv = jnp.where(lane == n, seen_sel, cntv)
            seen_eq = seen_eq + jnp.sum(eqf, axis=1, keepdims=True)
            seen_sel = seen_sel + jnp.sum(self_, axis=1, keepdims=True)
        if with_counts:
            cntv = jnp.where(lane == len(blocks), seen_sel, cntv)
            cnt_ref[0] = cntv.astype(I32)

    select(list(range(nlb)), cap_l, 0, True)
    select(list(range(nlb, nt)), cap_c, cap_l, False)


def _route_call(afft, tri, n_lat, cap_l, cap_c):
    b, nt, ne, _ = afft.shape
    return pl.pallas_call(
        functools.partial(_route_kernel, nlb=n_lat // TOKEN_BLOCK, cap_l=cap_l, cap_c=cap_c),
        grid=(b,),
        in_specs=[pl.BlockSpec((1, nt, ne, TOKEN_BLOCK), lambda bi: (bi, 0, 0, 0)),
                  pl.BlockSpec((TOKEN_BLOCK, TOKEN_BLOCK), lambda bi: (0, 0))],
        out_specs=[pl.BlockSpec((1, nt, ne, TOKEN_BLOCK), lambda bi: (bi, 0, 0, 0)),
                   pl.BlockSpec((1, ne, LANES), lambda bi: (bi, 0, 0))],
        out_shape=[jax.ShapeDtypeStruct(afft.shape, I32), jax.ShapeDtypeStruct((b, ne, LANES), I32)],
        compiler_params=_cparams(1),
        name="route",
    )(afft, tri)


def _slot_onehot(pos_row, first_slot, n_slots):
    slots = first_slot + lax.broadcasted_iota(I32, (n_slots, pos_row.shape[1]), 0)
    return jnp.where(pos_row == slots, 1.0, 0.0).astype(BF16)


def _gather_kernel(cnt_ref, h2_ref, pos_ref, xe_ref, *, nlb, cap_l, cap_c):
    for k in range(xe_ref.shape[1]):
        _gather_expert(k, cnt_ref, h2_ref, pos_ref, xe_ref, nlb=nlb, cap_l=cap_l, cap_c=cap_c)


def _gather_expert(k, cnt_ref, h2_ref, pos_ref, xe_ref, *, nlb, cap_l, cap_c):
    bi = pl.program_id(0)
    e = pl.program_id(1) * xe_ref.shape[1] + k
    ne = pl.num_programs(1) * xe_ref.shape[1]
    cbase = (bi * ne + e) * (nlb + 1)
    win = GATHER_WINDOW
    xe_ref[0, k, :cap_l, :] = jnp.zeros((cap_l, xe_ref.shape[3]), BF16)

    def slot_range(blk):
        return cnt_ref[cbase + blk], cnt_ref[cbase + blk + 1]

    def window_start(lo):
        return jnp.minimum((lo // BF16_ROWS) * BF16_ROWS, cap_l - win)

    def add_rows(blk, t0, s0, n):
        onehot = _slot_onehot(pos_ref[0, blk, pl.ds(e, 1), :], s0, n)
        xe_ref[0, k, pl.ds(s0, n), :] += jnp.dot(onehot, h2_ref[0, pl.ds(t0, TOKEN_BLOCK), :],
                                                 preferred_element_type=F32).astype(BF16)

    fits = jnp.bool_(True)
    for blk in range(nlb):
        lo, hi = slot_range(blk)
        fits = fits & (hi <= window_start(lo) + win)

    @pl.when(fits)
    def _():
        for blk in range(nlb):
            s0 = pl.multiple_of(window_start(slot_range(blk)[0]), BF16_ROWS)
            add_rows(blk, blk * TOKEN_BLOCK, s0, win)

    @pl.when(jnp.logical_not(fits))
    def _():
        def body(blk, _):
            lo, hi = slot_range(blk)

            def chunk(c, _):
                add_rows(blk, pl.multiple_of(blk * TOKEN_BLOCK, TOKEN_BLOCK),
                         pl.multiple_of(c * SLOT_CHUNK, SLOT_CHUNK), SLOT_CHUNK)
                return 0

            lax.fori_loop(lo // SLOT_CHUNK, jnp.where(hi > lo, (hi + SLOT_CHUNK - 1) // SLOT_CHUNK, 0), chunk, 0)
            return 0

        lax.fori_loop(0, nlb, body, 0)

    if cap_c:
        ctx_acc = jnp.zeros((cap_c, h2_ref.shape[2]), F32)
        for blk in range(nlb, pos_ref.shape[1]):
            onehot = _slot_onehot(pos_ref[0, blk, pl.ds(e, 1), :], cap_l, cap_c)
            t0 = blk * TOKEN_BLOCK
            ctx_acc = ctx_acc + jnp.dot(onehot, h2_ref[0, t0:t0 + TOKEN_BLOCK, :], preferred_element_type=F32)
        xe_ref[0, k, cap_l:cap_l + cap_c, :] = ctx_acc.astype(BF16)


def _gather_call(cnt_flat, h2, pos, n_lat, cap_l, cap_c):
    b, t, d = h2.shape
    _, nt, ne, _ = pos.shape
    slots = cap_l + cap_c
    eps = 2 if ne % 2 == 0 else 1
    return pl.pallas_call(
        functools.partial(_gather_kernel, nlb=n_lat // TOKEN_BLOCK, cap_l=cap_l, cap_c=cap_c),
        grid_spec=pltpu.PrefetchScalarGridSpec(
            num_scalar_prefetch=1, grid=(b, ne // eps),
            in_specs=[pl.BlockSpec((1, t, d), lambda bi, ei, s: (bi, 0, 0)),
                      pl.BlockSpec((1, nt, ne, TOKEN_BLOCK), lambda bi, ei, s: (bi, 0, 0, 0))],
            out_specs=pl.BlockSpec((1, eps, slots, d), lambda bi, ei, s: (bi, ei, 0, 0))),
        out_shape=jax.ShapeDtypeStruct((b, ne, slots, d), BF16),
        compiler_params=_cparams(2),
        name="moe_gather",
    )(cnt_flat, h2, pos)


def _ffn_kernel(x_ref, w1_ref, w3_ref, w2_ref, o_ref):
    ff = w1_ref.shape[-1]
    for s in range(x_ref.shape[0]):
        x = x_ref[s, 0]
        acc = None
        for f0 in range(0, ff, FF_CHUNK):
            f1 = min(ff, f0 + FF_CHUNK)
            a = jnp.dot(x, w1_ref[0, 0, :, f0:f1], preferred_element_type=F32)
            g = jnp.dot(x, w3_ref[0, 0, :, f0:f1], preferred_element_type=F32)
            hid = (a * jax.nn.sigmoid(a) * g).astype(BF16)
            part = jnp.dot(hid, w2_ref[0, 0, f0:f1, :], preferred_element_type=F32)
            acc = part if acc is None else acc + part
        o_ref[s, 0] = acc.astype(BF16)


def _ffn_call(layer, xe, w1, w3, w2):
    b, ne, slots, d = xe.shape
    ff = w1.shape[-1]
    spb = 2 if b % 2 == 0 else 1
    x_spec = pl.BlockSpec((spb, 1, slots, d), lambda ei, bi: (bi, ei, 0, 0))
    return pl.pallas_call(
        _ffn_kernel,
        grid=(ne, b // spb),
        in_specs=[x_spec,
                  pl.BlockSpec((1, 1, d, ff), lambda ei, bi: (layer, ei, 0, 0)),
                  pl.BlockSpec((1, 1, d, ff), lambda ei, bi: (layer, ei, 0, 0)),
                  pl.BlockSpec((1, 1, ff, d), lambda ei, bi: (layer, ei, 0, 0))],
        out_specs=x_spec,
        out_shape=jax.ShapeDtypeStruct(xe.shape, BF16),
        compiler_params=_cparams(2),
        name="moe_ffn",
    )(xe, w1, w3, w2)


def _combine_kernel(cnt_ref, x_ref, ye_ref, pos_ref, afft_ref, mod_ref, *rest, first_tile, nlb, cap_l, cap_c):
    xo_ref, acc_ref = rest[-2:]
    for sub in range(x_ref.shape[1] // TOKEN_BLOCK):
        _combine_tile(sub, first_tile, cnt_ref, x_ref, ye_ref, pos_ref, afft_ref, mod_ref, xo_ref, acc_ref,
                      nlb=nlb, cap_l=cap_l, cap_c=cap_c)


def _combine_tile(sub, first_tile, cnt_ref, x_ref, ye_ref, pos_ref, afft_ref, mod_ref, xo_ref, acc_ref, *,
                  nlb, cap_l, cap_c):
    bi = pl.program_id(0)
    ti = first_tile + pl.program_id(1) * (x_ref.shape[1] // TOKEN_BLOCK) + sub
    rows = slice(sub * TOKEN_BLOCK, (sub + 1) * TOKEN_BLOCK)
    ne = ye_ref.shape[1]
    is_lat = ti < nlb
    tn = (((0,), (0,)), ((), ()))
    win = GATHER_WINDOW

    def slot_range(e):
        cbase = (bi * ne + e) * (nlb + 1) + jnp.minimum(ti, nlb - 1)
        return cnt_ref[cbase], cnt_ref[cbase + 1]

    def window_start(lo):
        return jnp.minimum((lo // BF16_ROWS) * BF16_ROWS, cap_l - win)

    def gated_onehot(e, first_slot, n_slots):
        slots = first_slot + lax.broadcasted_iota(I32, (n_slots, TOKEN_BLOCK), 0)
        return jnp.where(pos_ref[0, sub, e:e + 1, :] == slots, afft_ref[0, sub, e:e + 1, :], 0.0).astype(BF16)

    def residual(y):
        xo_ref[0, rows] = x_ref[0, rows] + mod_ref[0, 0][5:6] * y

    def scatter(pieces):
        w = jnp.concatenate([p[0] for p in pieces], axis=0)
        y = jnp.concatenate([p[1] for p in pieces], axis=0)
        residual(lax.dot_general(w, y, tn, preferred_element_type=F32))

    if first_tile >= nlb:
        scatter([(gated_onehot(e, cap_l, cap_c), ye_ref[0, e, :cap_c, :]) for e in range(ne)])
        return

    fits = is_lat
    for e in range(ne):
        lo, hi = slot_range(e)
        fits = fits & (hi <= window_start(lo) + win)

    @pl.when(fits)
    def _():
        pieces = []
        for e in range(ne):
            s0 = pl.multiple_of(window_start(slot_range(e)[0]), BF16_ROWS)
            pieces.append((gated_onehot(e, s0, win), ye_ref[0, e, pl.ds(s0, win), :]))
        scatter(pieces)

    @pl.when(is_lat & jnp.logical_not(fits))
    def _():
        acc_ref[...] = jnp.zeros_like(acc_ref)
        for e in range(ne):
            lo, hi = slot_range(e)

            def body(c, _, e=e):
                s0 = pl.multiple_of(c * SLOT_CHUNK, SLOT_CHUNK)
                acc_ref[...] += lax.dot_general(gated_onehot(e, s0, SLOT_CHUNK),
                                                ye_ref[0, e, pl.ds(s0, SLOT_CHUNK), :], tn,
                                                preferred_element_type=F32)
                return 0

            lax.fori_loop(lo // SLOT_CHUNK, jnp.where(hi > lo, (hi + SLOT_CHUNK - 1) // SLOT_CHUNK, 0), body, 0)
        residual(acc_ref[...])


def _combine_call(cnt_flat, xs, ye, pos, afft, modt, n_lat, cap_l, cap_c, latent_only):
    b, t, d = xs.shape
    _, nt, ne, _ = pos.shape
    nlb = n_lat // TOKEN_BLOCK
    n_ctx = t - n_lat
    out_rows = n_lat if latent_only else t
    assert cap_c == 0 or cap_l % cap_c == 0

    def call(rows, first_block, n_steps, stream, prev):
        nsub = rows // TOKEN_BLOCK
        tok_spec = pl.BlockSpec((1, rows, d), lambda bi, ti, s: (bi, first_block + ti, 0))
        blk_spec = pl.BlockSpec((1, nsub, ne, TOKEN_BLOCK), lambda bi, ti, s: (bi, first_block + ti, 0, 0))
        if stream == 0:
            ye_spec = pl.BlockSpec((1, ne, cap_l, d), lambda bi, ti, s: (bi, 0, 0, 0))
        else:
            ye_spec = pl.BlockSpec((1, ne, cap_c, d), lambda bi, ti, s: (bi, 0, cap_l // cap_c, 0))
        operands = (cnt_flat, xs, ye, pos, afft, modt) + tuple(prev)
        return pl.pallas_call(
            functools.partial(_combine_kernel, first_tile=first_block * nsub, nlb=nlb, cap_l=cap_l, cap_c=cap_c),
            grid_spec=pltpu.PrefetchScalarGridSpec(
                num_scalar_prefetch=1, grid=(b, n_steps),
                in_specs=[tok_spec, ye_spec, blk_spec, blk_spec,
                          pl.BlockSpec((1, 1, 6, d), lambda bi, ti, s: (bi, stream, 0, 0))]
                         + [pl.BlockSpec(memory_space=pl.ANY)] * len(prev),
                out_specs=tok_spec,
                scratch_shapes=[pltpu.VMEM((TOKEN_BLOCK, d), F32)]),
            out_shape=jax.ShapeDtypeStruct((b, out_rows, d), F32),
            input_output_aliases={6 + n: n for n in range(len(prev))},
            compiler_params=_cparams(2),
            name="moe_combine",
        )(*operands)

    rows = next(u for u in (2 * TOKEN_BLOCK, TOKEN_BLOCK) if n_lat % u == 0)
    out = call(rows, 0, n_lat // rows, 0, ())
    return out if latent_only else call(n_ctx, n_lat // n_ctx, 1, 1, (out,))


def _rope_tables(n_lat, n_ctx):
    rows = n_lat // GRID_W
    row = jnp.repeat(jnp.arange(rows), GRID_W).astype(F32)
    col = jnp.tile(jnp.arange(GRID_W), rows).astype(F32)
    half = HEAD_DIM // 2
    inv_freq = ROPE_THETA ** (-jnp.arange(0, half, 2, dtype=F32) / half)
    ar = row[:, None] * inv_freq
    ac = col[:, None] * inv_freq
    cos64 = jnp.concatenate([jnp.cos(ar), jnp.cos(ar), jnp.cos(ac), jnp.cos(ac)], axis=1)
    sin64 = jnp.concatenate([-jnp.sin(ar), jnp.sin(ar), -jnp.sin(ac), jnp.sin(ac)], axis=1)
    cos_t = jnp.concatenate([jnp.tile(cos64, (1, LANES // HEAD_DIM)), jnp.ones((n_ctx, LANES), F32)], axis=0)
    sin_t = jnp.concatenate([jnp.tile(sin64, (1, LANES // HEAD_DIM)), jnp.zeros((n_ctx, LANES), F32)], axis=0)
    return cos_t, sin_t


def _swa_head_order():
    heads = []
    for j in range(SWA_KV_HEADS // 2):
        for g in range(SWA_GROUP):
            heads += [2 * SWA_GROUP * j + g, 2 * SWA_GROUP * j + SWA_GROUP + g]
    return np.concatenate([np.arange(h * HEAD_DIM, (h + 1) * HEAD_DIM) for h in heads])


def kernel(x, c, ctx, c_ctx, w_ada, b_ada, norm1_g, w_in, b_gate, diff_q_g, diff_k_g, diff_lambda, diff_subln_g,
           swa_q_g, swa_k_g, swa_sink, w_branch_a, w_branch_b, w_out, norm2_g, w_router, w_e1, w_e3, w_e2):
    b, n_lat, d = x.shape
    n_ctx = ctx.shape[1]
    depth = w_ada.shape[0]
    assert n_lat % TOKEN_BLOCK == 0 and n_ctx % TOKEN_BLOCK == 0 and n_lat >= 2 * TOKEN_BLOCK
    assert d == DIFF_HEADS * 2 * HEAD_DIM == SWA_Q_HEADS * HEAD_DIM and w_router.shape[-1] == N_EXPERTS
    cap_l = CAPACITY_FACTOR * n_lat // N_EXPERTS
    cap_c = CAPACITY_FACTOR * n_ctx // N_EXPERTS
    assert cap_l % SLOT_CHUNK == 0 and cap_c % 16 == 0
    nlb = n_lat // TOKEN_BLOCK

    perm = _swa_head_order()
    kvw = SWA_KV_HEADS * HEAD_DIM
    o = np.cumsum([0, d, d, d, d, kvw, kvw, d, d])
    w_in_p = jnp.concatenate(
        [w_in[..., o[0]:o[3]], w_in[..., o[3]:o[4]][..., perm], w_in[..., o[6]:o[8]], w_in[..., o[4]:o[6]]],
        axis=-1).astype(BF16)
    wa = w_branch_a.astype(BF16)
    wb = w_branch_b[:, perm, :].astype(BF16)
    wo = w_out.astype(BF16)
    wrt =jnp.swapaxes(w_router, 1, 2).astype(BF16)
    w1 = w_e1.astype(BF16)
    w3 = w_e3.astype(BF16)
    w2 = w_e2.astype(BF16)
    scale = HEAD_DIM ** -0.5
    hg = jnp.stack([jnp.tile(diff_q_g, (1, d // HEAD_DIM)) * (scale * math.log2(math.e)),
                    jnp.tile(diff_k_g, (1, d // HEAD_DIM)),
                    jnp.tile(swa_q_g, (1, d // HEAD_DIM)) * (scale * math.log2(math.e)),
                    jnp.tile(swa_k_g, (1, d // HEAD_DIM))], axis=1)
    def score_bound(q_gain, k_gain):
        return (HEAD_DIM * scale * math.log2(math.e)) * jnp.max(jnp.abs(q_gain), axis=1) * jnp.max(jnp.abs(k_gain), axis=1)

    diff_bounded = (score_bound(diff_q_g, diff_k_g) <= SCORE_BOUND_LOG2).astype(I32)
    swa_bounded = ((score_bound(swa_q_g, swa_k_g) <= SCORE_BOUND_LOG2)
                   & (jnp.max(swa_sink, axis=1) * math.log2(math.e) <= SCORE_BOUND_LOG2)).astype(I32)
    subln = jnp.broadcast_to(diff_subln_g[:, :, None], (depth, LANES, LANES))
    g1n = norm1_g.reshape(depth, 1, d)
    g2n = norm2_g.reshape(depth, 1, d)
    bg = b_gate.reshape(depth, 1, 2 * d)
    gm = jnp.asarray(np.kron(np.eye(TOKEN_BLOCK // HEAD_DIM), np.ones((HEAD_DIM, HEAD_DIM))), BF16)
    tri = jnp.asarray(np.triu(np.ones((TOKEN_BLOCK, TOKEN_BLOCK)), 1), BF16)
    cos_t, sin_t = _rope_tables(n_lat, n_ctx)
    key_i = np.arange(2 * TOKEN_BLOCK)[:, None]
    qry_i = np.arange(TOKEN_BLOCK)[None, :]
    band = np.where(np.abs(qry_i + WINDOW - key_i) <= WINDOW, 0.0, NEG_INF)
    band = jnp.asarray(np.concatenate([band, np.zeros((n_ctx, TOKEN_BLOCK))], axis=0), BF16)

    rows = -(-(b + 1) // 8) * 8
    cc = jnp.concatenate([c, c_ctx[None], jnp.zeros((rows - b - 1, d), F32)], axis=0)
    mod_all = _ada_call(cc, w_ada, b_ada)

    xs = jnp.concatenate([x, ctx], axis=1)
    for i in range(depth):
        lam_init = 0.8 - 0.6 * math.exp(-0.3 * i)
        mod_l = mod_all[i, :b].reshape(b, 1, 6, d)
        mod_c = jnp.broadcast_to(mod_all[i, b].reshape(1, 1, 6, d), (b, 1, 6, d))
        modt = jnp.concatenate([mod_l, mod_c], axis=1)
        qa, ka, va, qb, gates, kb, vb = _inproj_call(i, xs, modt, g1n, w_in_p, gm, hg, bg, cos_t, sin_t, n_lat)
        ya = _diff_attn_call(i, diff_bounded[i:i + 1], qa, ka, va, diff_lambda, subln, n_lat, lam_init)
        yb = _swa_call(i, swa_bounded[i:i + 1], swa_sink[i], qb, kb, vb, band, n_lat)
        xs, h2, afft = _merge_call(i, xs, ya, yb, gates, modt, wa, wb, wo, g2n, wrt, n_lat)
        pos, cnt = _route_call(afft, tri, n_lat, cap_l, cap_c)
        cnt_flat = cnt[:, :, :nlb + 1].reshape(-1)
        last = i == depth - 1
        cap_ctx = 0 if last else cap_c
        xe = _gather_call(cnt_flat, h2, pos, n_lat, cap_l, cap_ctx)
        ye = _ffn_call(i, xe, w1, w3, w2)
        xs = _combine_call(cnt_flat, xs, ye, pos, afft, modt, n_lat, cap_l, cap_ctx, latent_only=last)
    return xs
```

```python
import functools
import math

import numpy as np
import jax
import jax.numpy as jnp
from jax import lax
from jax.experimental import pallas as pl
from jax.experimental.pallas import tpu as pltpu

F32 = jnp.float32
BF16 = jnp.bfloat16
I32 = jnp.int32

HEAD_DIM = 64
GRID_W = 64
DIFF_HEADS = 8
SWA_Q_HEADS = 16
SWA_KV_HEADS = 4
SWA_GROUP = SWA_Q_HEADS // SWA_KV_HEADS
WINDOW = 128
N_EXPERTS = 16
CAPACITY_FACTOR = 2
ROPE_THETA = 10000.0
EPS = 1e-6
NEG_INF = -1e30

LANES = 128
TOKEN_BLOCK = 256
SLOT_CHUNK = 128
BF16_ROWS = 16
SCORE_BOUND_LOG2 = 48.0
GATHER_WINDOW = 64
PROJ_CHUNK = 512
FF_CHUNK = 1536
VT_ROWS = LANES + 16
VMEM_LIMIT = 56 * 1024 * 1024


def _cparams(n_axes):
    return pltpu.CompilerParams(dimension_semantics=("arbitrary",) * n_axes, vmem_limit_bytes=VMEM_LIMIT)


def _ada_kernel(c_ref, w_ref, b_ref, o_ref):
    c = c_ref[...]
    sc = c * jax.nn.sigmoid(c)
    o_ref[0] = jnp.dot(sc, w_ref[0], preferred_element_type=F32) + b_ref[0]


def _ada_call(cc, w_ada, b_ada):
    depth, d, six_d = w_ada.shape
    rows = cc.shape[0]
    cols = 1536
    return pl.pallas_call(
        _ada_kernel,
        grid=(depth, six_d // cols),
        in_specs=[pl.BlockSpec((rows, d), lambda i, j: (0, 0)),
                  pl.BlockSpec((1, d, cols), lambda i, j: (i, 0, j)),
                  pl.BlockSpec((1, 1, cols), lambda i, j: (i, 0, j))],
        out_specs=pl.BlockSpec((1, rows, cols), lambda i, j: (i, 0, j)),
        out_shape=jax.ShapeDtypeStruct((depth, rows, six_d), F32),
        compiler_params=_cparams(2),
        name="ada",
    )(cc, w_ada, b_ada.reshape(depth, 1, six_d))


def _inproj_kernel(x_ref, mod_ref, g1_ref, w_ref, gm_ref, hg_ref, bg_ref, cos_ref, sin_ref, *rest, d):
    qa_ref, ka_ref, va_ref, qb_ref, gate_ref, kb_ref, vb_ref = rest[-7:]
    mod = mod_ref[0, 0]
    lane = lax.broadcasted_iota(I32, (1, LANES), 1)
    first_half = (lane % 32) < 16
    gm = gm_ref[...]
    per = PROJ_CHUNK // LANES
    kvw = SWA_KV_HEADS * HEAD_DIM
    ones_rows = jnp.ones((VT_ROWS - LANES, TOKEN_BLOCK), BF16)

    for sub in range(x_ref.shape[1] // TOKEN_BLOCK):
        rows = slice(sub * TOKEN_BLOCK, (sub + 1) * TOKEN_BLOCK)
        x = x_ref[0, rows]
        ms = jnp.mean(x * x, axis=-1, keepdims=True)
        h = x * lax.rsqrt(ms + EPS) * g1_ref[0]
        h = (h * (1.0 + mod[1:2]) + mod[0:1]).astype(BF16)
        cos = cos_ref[rows]
        sin = sin_ref[rows]

        def proj(c0, width, h=h):
            return jnp.dot(h, w_ref[0, :, c0:c0 + width], preferred_element_type=F32)

        def head_norm_rope(p, gain, cos=cos, sin=sin):
            w = p.shape[1]
            gw = gm.shape[0]
            sq = (p * p).astype(BF16)
            msq = jnp.concatenate([jnp.dot(sq[:, g0:g0 + gw], gm, preferred_element_type=F32)
                                   for g0 in range(0, w, gw)], axis=1) * (1.0 / HEAD_DIM)
            qn = p * lax.rsqrt(msq + EPS) * gain
            outs = []
            for u in range(w // LANES):
                seg = qn[:, u * LANES:(u + 1) * LANES]
                partner = jnp.where(first_half, pltpu.roll(seg, LANES - 16, 1), pltpu.roll(seg, 16, 1))
                outs.append((seg * cos + partner * sin).astype(BF16))
            return outs

        for sec, (out_ref, gain_row) in enumerate(((qa_ref, 0), (ka_ref, 1))):
            for c in range(d // PROJ_CHUNK):
                cols = slice(c * PROJ_CHUNK, (c + 1) * PROJ_CHUNK)
                outs = head_norm_rope(proj(sec * d + c * PROJ_CHUNK, PROJ_CHUNK), hg_ref[0, gain_row:gain_row + 1, cols])
                for u, o in enumerate(outs):
                    out_ref[0, c * per + u, rows] = o
        for c in range(d // PROJ_CHUNK):
            p = proj(2 * d + c * PROJ_CHUNK, PROJ_CHUNK)
            for u in range(per):
                va_ref[0, c * per + u, sub, :LANES, :] = p[:, u * LANES:(u + 1) * LANES].T.astype(BF16)
                va_ref[0, c * per + u, sub, LANES:, :] = ones_rows
        for c in range(d // PROJ_CHUNK):
            cols = slice(c * PROJ_CHUNK, (c + 1) * PROJ_CHUNK)
            outs = head_norm_rope(proj(3 * d + c * PROJ_CHUNK, PROJ_CHUNK), hg_ref[0, 2:3, cols])
            for u, o in enumerate(outs):
                qb_ref[0, c * per + u, rows] = o
        outs = head_norm_rope(proj(6 * d, kvw), hg_ref[0, 3:4, :kvw])
        for u, o in enumerate(outs):
            kb_ref[0, u, rows] = o
        p = proj(6 * d + kvw, kvw)
        for u in range(kvw // LANES):
            vb_ref[0, u, sub, :LANES, :] = p[:, u * LANES:(u + 1) * LANES].T.astype(BF16)
            vb_ref[0, u, sub, LANES:, :] = ones_rows
        for c in range(2 * d // PROJ_CHUNK):
            cols = slice(c * PROJ_CHUNK, (c + 1) * PROJ_CHUNK)
            p = proj(4 * d + c * PROJ_CHUNK, PROJ_CHUNK) + bg_ref[0, :, cols]
            gate_ref[0, rows, cols] = jax.nn.sigmoid(p).astype(BF16)


def _stream_shape(xs):
    if isinstance(xs, tuple):
        return xs[0].shape[0], xs[0].shape[1] + xs[1].shape[1], xs[0].shape[2]
    return xs.shape


def _stream_part(xs, stream, first_block):
    return (xs[stream], 0) if isinstance(xs, tuple) else (xs, first_block)


def _inproj_call(layer, xs, modt, norm1_g, w_in_p, gm, hg, bg, cos_t, sin_t, n_lat):
    b, t, d = _stream_shape(xs)
    nt = t // TOKEN_BLOCK
    n_ctx = t - n_lat
    assert n_lat % n_ctx == 0
    in_w = w_in_p.shape[-1]
    nh = d // LANES
    nkv = SWA_KV_HEADS * HEAD_DIM // LANES
    out_shape = [jax.ShapeDtypeStruct((b, nh, t, LANES), BF16), jax.ShapeDtypeStruct((b, nh, t, LANES), BF16),
                 jax.ShapeDtypeStruct((b, nh, nt, VT_ROWS, TOKEN_BLOCK), BF16),
                 jax.ShapeDtypeStruct((b, nh, t, LANES), BF16), jax.ShapeDtypeStruct((b, t, 2 * d), BF16),
                 jax.ShapeDtypeStruct((b, nkv, t, LANES), BF16),
                 jax.ShapeDtypeStruct((b, nkv, nt, VT_ROWS, TOKEN_BLOCK), BF16)]

    def call(rows, first_block, n_steps, stream, prev):
        nsub = rows // TOKEN_BLOCK

        def tok(bi, ti):
            return (bi, first_block + ti, 0)

        head_spec = pl.BlockSpec((1, nh, rows, LANES), lambda bi, ti: (bi, 0, first_block + ti, 0))
        kv_spec = pl.BlockSpec((1, nkv, rows, LANES), lambda bi, ti: (bi, 0, first_block + ti, 0))
        vt_spec = pl.BlockSpec((1, nh, nsub, VT_ROWS, TOKEN_BLOCK), lambda bi, ti: (bi, 0, first_block + ti, 0, 0))
        kvt_spec = pl.BlockSpec((1, nkv, nsub, VT_ROWS, TOKEN_BLOCK), lambda bi, ti: (bi, 0, first_block + ti, 0, 0))
        x_arr, x_first = _stream_part(xs, stream, first_block)
        operands = (x_arr, modt, norm1_g, w_in_p, gm, hg, bg, cos_t, sin_t) + tuple(prev)
        return pl.pallas_call(
            functools.partial(_inproj_kernel, d=d),
            grid=(b, n_steps),
            in_specs=[pl.BlockSpec((1, rows, d), lambda bi, ti: (bi, x_first + ti, 0)),
                      pl.BlockSpec((1, 1, 6, d), lambda bi, ti: (bi, stream, 0, 0)),
                      pl.BlockSpec((1, 1, d), lambda bi, ti: (layer, 0, 0)),
                      pl.BlockSpec((1, d, in_w), lambda bi, ti: (layer, 0, 0)),
                      pl.BlockSpec(gm.shape, lambda bi, ti: (0, 0)),
                      pl.BlockSpec((1, 4, d), lambda bi, ti: (layer, 0, 0)),
                      pl.BlockSpec((1, 1, 2 * d), lambda bi, ti: (layer, 0, 0)),
                      pl.BlockSpec((rows, LANES), lambda bi, ti: (first_block + ti, 0)),
                      pl.BlockSpec((rows, LANES), lambda bi, ti: (first_block + ti, 0))]
                     + [pl.BlockSpec(memory_space=pl.ANY)] * len(prev),
            out_specs=[head_spec, head_spec, vt_spec, head_spec, pl.BlockSpec((1, rows, 2 * d), tok),
                       kv_spec, kvt_spec],
            out_shape=out_shape,
            input_output_aliases={9 + n: n for n in range(len(prev))},
            compiler_params=_cparams(2),
            name="inproj",
        )(*operands)

    rows = next(u for u in (2 * TOKEN_BLOCK, TOKEN_BLOCK) if n_lat % u == 0)
    outs = call(rows, 0, n_lat // rows, 0, ())
    return call(n_ctx, n_lat // n_ctx, 1, 1, outs)


def _diff_attn_kernel(flag_ref, q_ref, k_ref, v_ref, dl_ref, sg_ref, *rest, key_blocks, bpu, lam_init):
    o_ref = rest[-1]
    nq = 2 * q_ref.shape[2]
    nn_t = (((1,), (1,)), ((), ()))

    def stacked_queries(hh):
        q = q_ref[0, hh]
        lane = lax.broadcasted_iota(I32, q.shape, 1)
        zero = jnp.zeros_like(q)
        return jnp.concatenate([jnp.where(lane < HEAD_DIM, q, zero), jnp.where(lane >= HEAD_DIM, q, zero)], axis=0)

    def scores(hq, blk):
        hh, q2 = hq
        k = k_ref[0, hh, blk * TOKEN_BLOCK:(blk + 1) * TOKEN_BLOCK, :]
        return lax.dot_general(k, q2, nn_t, preferred_element_type=F32).astype(BF16)

    def process(hq, blocks, s_blocks, state, next_blocks):
        m, l, acc = state
        m_blk = s_blocks[0]
        for s in s_blocks[1:]:
            m_blk = jnp.maximum(m_blk, s)
        m_new = jnp.maximum(m, jnp.max(m_blk, axis=0, keepdims=True).astype(F32))
        alpha = jnp.exp2(m - m_new)
        m_bf = m_new.astype(BF16)
        s_next = []
        r = None
        for c, (blk, s) in enumerate(zip(blocks, s_blocks)):
            if c < len(next_blocks):
                s_next.append(scores(hq, next_blocks[c]))
            p = jnp.exp2(s - m_bf)
            part = jnp.dot(v_ref[0, hq[0], blk], p, preferred_element_type=F32)
            r = part if r is None else r + part
        s_next += [scores(hq, blk) for blk in next_blocks[len(blocks):]]
        return (m_new, alpha * l + r[LANES:LANES + 1], alpha * acc + r[:LANES]), s_next

    init = (jnp.full((1, nq), NEG_INF, F32), jnp.zeros((1, nq), F32), jnp.zeros((LANES, nq), F32))

    def finish(hh, state):
        _, l, acc = state
        o = acc / l
        dl = dl_ref[0]
        lam = (jnp.exp(jnp.sum(dl[0:1] * dl[1:2], axis=1, keepdims=True))
               - jnp.exp(jnp.sum(dl[2:3] * dl[3:4], axis=1, keepdims=True)) + lam_init)
        y = o[:, :nq // 2] - lam * o[:, nq // 2:]
        y = y * lax.rsqrt(jnp.mean(y * y, axis=0, keepdims=True) + EPS) * (sg_ref[0][:, 0:1] * (1.0 - lam_init))
        o_ref[0, hh] = y.astype(BF16)

    def bounded(hh):
        hq = (hh, stacked_queries(hh))
        r = None
        s = scores(hq, key_blocks[0])
        for n, blk in enumerate(key_blocks):
            s_next = scores(hq, key_blocks[n + 1]) if n + 1 < len(key_blocks) else None
            part = jnp.dot(v_ref[0, hh, blk], jnp.exp2(s), preferred_element_type=F32)
            r = part if r is None else r + part
            s = s_next
        finish(hh, (None, r[LANES:LANES + 1], r[:LANES]))

    def online(hh):
        hq = (hh, stacked_queries(hh))
        units = [key_blocks[u:u + bpu] for u in range(0, len(key_blocks), bpu)]
        state = init
        s_cur = [scores(hq, blk) for blk in units[0]]
        for u, blocks in enumerate(units):
            state, s_cur = process(hq, blocks, s_cur, state, units[u + 1] if u + 1 < len(units) else [])
        finish(hh, state)

    @pl.when(flag_ref[0] > 0)
    def _():
        for hh in range(q_ref.shape[1]):
            bounded(hh)

    @pl.when(flag_ref[0] <= 0)
    def _():
        for hh in range(q_ref.shape[1]):
            online(hh)


def _diff_attn_call(layer, flag, qa, ka, vat, diff_lambda, subln_g, n_lat, lam_init):
    b, nh, t, _ = qa.shape
    nt = t // TOKEN_BLOCK
    nlb = n_lat // TOKEN_BLOCK
    n_ctx = t - n_lat
    assert n_lat % n_ctx == 0
    tq = next(u for u in (2 * TOKEN_BLOCK, TOKEN_BLOCK) if n_lat % u == 0)
    hps = 2 if nh % 2 == 0 else 1
    par_specs = [pl.BlockSpec((1, 4, HEAD_DIM), lambda bi, hi, ti, f: (layer, 0, 0)),
                 pl.BlockSpec((1, LANES, LANES), lambda bi, hi, ti, f: (layer, 0, 0))]

    def call(rows, first_block, n_steps, key_blocks, prev):
        nkb = len(key_blocks)
        k_spec = pl.BlockSpec((1, hps, nkb * TOKEN_BLOCK, LANES), lambda bi, hi, ti, f: (bi, hi, key_blocks[0] // nkb, 0))
        v_spec = pl.BlockSpec((1, hps, nkb, VT_ROWS, TOKEN_BLOCK),
                              lambda bi, hi, ti, f: (bi, hi, key_blocks[0] // nkb, 0, 0))
        q_spec = pl.BlockSpec((1, hps, rows, LANES), lambda bi, hi, ti, f: (bi, hi, first_block + ti, 0))
        o_spec = pl.BlockSpec((1, hps, LANES, rows), lambda bi, hi, ti, f: (bi, hi, 0, first_block + ti))
        extra = [] if prev is None else [pl.BlockSpec(memory_space=pl.ANY)]
        operands = (flag, qa, ka, vat, diff_lambda, subln_g) + (() if prev is None else (prev,))
        return pl.pallas_call(
            functools.partial(_diff_attn_kernel, key_blocks=list(range(nkb)), bpu=4, lam_init=lam_init),
            grid_spec=pltpu.PrefetchScalarGridSpec(
                num_scalar_prefetch=1, grid=(b, nh // hps, n_steps),
                in_specs=[q_spec, k_spec, v_spec] + par_specs + extra, out_specs=o_spec),
            out_shape=jax.ShapeDtypeStruct((b, nh, LANES, t), BF16),
            input_output_aliases={} if prev is None else {len(operands) - 1: 0},
            compiler_params=_cparams(3),
            name="diff_attn",
        )(*operands)

    ya = call(tq, 0, n_lat // tq, list(range(nt)), None)
    return call(n_ctx, n_lat // n_ctx, 1, list(range(nlb, nt)), ya)


def _swa_kernel(flag_ref, sink_ref, q_ref, k_ref, v_ref, band_ref, o_ref, *, nlb):
    i = pl.program_id(1)
    npair = k_ref.shape[1]
    nt = v_ref.shape[2]
    n_lat = nlb * TOKEN_BLOCK
    is_lat = i < nlb
    im1 = jnp.maximum(i - 1, 0)
    ip1 = jnp.minimum(i + 1, nt - 1)
    nk = 2 * TOKEN_BLOCK + (nt - nlb) * TOKEN_BLOCK

    n_chunks = nk // TOKEN_BLOCK

    def key_chunk(j, c):
        def krows(blk, lo):
            return k_ref[0, j, pl.ds(pl.multiple_of(blk * TOKEN_BLOCK + lo, WINDOW), WINDOW), :]
        if c == 0:
            return jnp.concatenate([krows(im1, WINDOW), krows(i, 0)], axis=0)
        if c == 1:
            return jnp.concatenate([krows(i, WINDOW), krows(ip1, 0)], axis=0)
        return k_ref[0, j, (nlb + c - 2) * TOKEN_BLOCK:(nlb + c - 1) * TOKEN_BLOCK, :]

    def value_chunk(j, half, c):
        if c == 0:
            v = jnp.concatenate([v_ref[0, j, im1][:, WINDOW:], v_ref[0, j, i][:, :WINDOW]], axis=1)
        elif c == 1:
            v = jnp.concatenate([v_ref[0, j, i][:, WINDOW:], v_ref[0, j, ip1][:, :WINDOW]], axis=1)
        else:
            v = v_ref[0, j, nlb + c - 2]
        return jnp.concatenate([v[half * HEAD_DIM:(half + 1) * HEAD_DIM], v[LANES:]], axis=0)
    neg = jnp.float32(NEG_INF)
    zero = jnp.float32(0.0)
    pen = (jnp.where((i == 0) | jnp.logical_not(is_lat), neg, zero), jnp.where(is_lat, zero, neg),
           jnp.where(i >= nlb - 1, neg, zero))
    part = jnp.concatenate([jnp.full((WINDOW, TOKEN_BLOCK), pen[0], F32), jnp.full((TOKEN_BLOCK, TOKEN_BLOCK), pen[1], F32),
                            jnp.full((WINDOW, TOKEN_BLOCK), pen[2], F32),
                            jnp.zeros((nk - 2 * TOKEN_BLOCK, TOKEN_BLOCK), F32)], axis=0)
    bias1 = band_ref[...] + part.astype(BF16)
    bias = jnp.concatenate([bias1] * SWA_GROUP, axis=1)
    lane = lax.broadcasted_iota(I32, (TOKEN_BLOCK, LANES), 1)
    nn_t = (((1,), (1,)), ((), ()))
    log2e = math.log2(math.e)

    kv_heads = [(j, half) for j in range(npair) for half in range(2)]

    def queries(j, half):
        in_half = (lane >= half * HEAD_DIM) & (lane < (half + 1) * HEAD_DIM)
        return jnp.concatenate([jnp.where(in_half, q_ref[0, SWA_GROUP * j + g], jnp.zeros((TOKEN_BLOCK, LANES), BF16))
                                for g in range(SWA_GROUP)], axis=0)

    def scores(j, qs, c):
        s = lax.dot_general(key_chunk(j, c), qs, nn_t, preferred_element_type=F32).astype(BF16)
        return s + bias[c * TOKEN_BLOCK:(c + 1) * TOKEN_BLOCK]

    def sinks(j, half):
        return jnp.concatenate(
            [jnp.full((1, TOKEN_BLOCK), sink_ref[2 * SWA_GROUP * j + SWA_GROUP * half + g] * log2e, F32)
             for g in range(SWA_GROUP)], axis=1)

    def store(j, o0, o1):
        for g in range(SWA_GROUP):
            sl = slice(g * TOKEN_BLOCK, (g + 1) * TOKEN_BLOCK)
            o_ref[0, SWA_GROUP * j + g] = jnp.concatenate([o0[:, sl], o1[:, sl]], axis=0).astype(BF16)

    @pl.when(flag_ref[0] > 0)
    def _():
        steps = [(h, c) for h in range(len(kv_heads)) for c in range(n_chunks)]
        qs = [queries(j, half) for j, half in kv_heads]
        r = [None] * len(kv_heads)
        s = scores(kv_heads[0][0], qs[0], 0)
        for n, (h, c) in enumerate(steps):
            s_next = None
            if n + 1 < len(steps):
                h2, c2 = steps[n + 1]
                s_next = scores(kv_heads[h2][0], qs[h2], c2)
            part = jnp.dot(value_chunk(*kv_heads[h], c), jnp.exp2(s), preferred_element_type=F32)
            r[h] = part if r[h] is None else r[h] + part
            s = s_next
        outs = [r[h][:HEAD_DIM] / (r[h][HEAD_DIM:HEAD_DIM + 1] + jnp.exp2(sinks(*kv_heads[h])))
                for h in range(len(kv_heads))]
        for j in range(npair):
            store(j, outs[2 * j], outs[2 * j + 1])

    @pl.when(flag_ref[0] <= 0)
    def _():
        for j in range(npair):
            outs = []
            for half in range(2):
                qs = queries(j, half)
                s = jnp.concatenate([scores(j, qs, c) for c in range(n_chunks)], axis=0)
                sink = sinks(j, half)
                m = jnp.maximum(jnp.max(s, axis=0, keepdims=True).astype(F32), sink).astype(BF16)
                vh = jnp.concatenate([value_chunk(j, half, c) for c in range(n_chunks)], axis=1)
                r = jnp.dot(vh, jnp.exp2(s - m), preferred_element_type=F32)
                outs.append(r[:HEAD_DIM] / (r[HEAD_DIM:HEAD_DIM + 1] + jnp.exp2(sink - m.astype(F32))))
            store(j, *outs)


def _swa_call(layer, flag, sink, qb, kb, vbt, band, n_lat):
    b, nslab, t, _ = qb.shape
    nt = t // TOKEN_BLOCK
    npair = kb.shape[1]
    assert 2 * WINDOW == TOKEN_BLOCK
    q_spec = pl.BlockSpec((1, nslab, TOKEN_BLOCK, LANES), lambda bi, ti, f, s: (bi, 0, ti, 0))
    k_spec = pl.BlockSpec((1, npair, t, LANES), lambda bi, ti, f, s: (bi, 0, 0, 0))
    v_spec = pl.BlockSpec((1, npair, nt, VT_ROWS, TOKEN_BLOCK), lambda bi, ti, f, s: (bi, 0, 0, 0, 0))
    return pl.pallas_call(
        functools.partial(_swa_kernel, nlb=n_lat // TOKEN_BLOCK),
        grid_spec=pltpu.PrefetchScalarGridSpec(
            num_scalar_prefetch=2, grid=(b, nt),
            in_specs=[q_spec, k_spec, v_spec, pl.BlockSpec(band.shape, lambda bi, ti, f, s: (0, 0))],
            out_specs=pl.BlockSpec((1, nslab, LANES, TOKEN_BLOCK), lambda bi, ti, f, s: (bi, 0, 0, ti))),
        out_shape=jax.ShapeDtypeStruct((b, nslab, LANES, t), BF16),
        compiler_params=_cparams(2),
        name="swa_attn",
    )(flag, sink, qb, kb, vbt, band)


def _merge_kernel(x_ref, ya_ref, yb_ref, gate_ref, mod_ref, wa_ref, wb_ref, wo_ref, g2_ref, wrt_ref, *rest, d):
    xo_ref, h2_ref, afft_ref = rest[-3:]
    nh = ya_ref.shape[1]
    tn = (((0,), (0,)), ((), ()))
    mod = mod_ref[0, 0]
    for sub in range(x_ref.shape[1] // TOKEN_BLOCK):
        rows = slice(sub * TOKEN_BLOCK, (sub + 1) * TOKEN_BLOCK)
        ya = jnp.concatenate([ya_ref[0, h, :, rows] for h in range(nh)], axis=0)
        yb = jnp.concatenate([yb_ref[0, h, :, rows] for h in range(nh)], axis=0)
        za = lax.dot_general(ya, wa_ref[0], tn, preferred_element_type=F32)
        zb = lax.dot_general(yb, wb_ref[0], tn, preferred_element_type=F32)
        gate = gate_ref[0, rows].astype(F32)
        u = gate[:, :d] * za + gate[:, d:] * zb
        z = jnp.dot(u.astype(BF16), wo_ref[0], preferred_element_type=F32)
        xn = x_ref[0, rows] + mod[2:3] * z
        xo_ref[0, rows] = xn
        ms = jnp.mean(xn * xn, axis=-1, keepdims=True)
        h2 = xn * lax.rsqrt(ms + EPS) * g2_ref[0]
        h2 = (h2 * (1.0 + mod[4:5]) + mod[3:4]).astype(BF16)
        h2_ref[0, rows] = h2
        lgt = lax.dot_general(wrt_ref[0], h2, (((1,), (1,)), ((), ())), preferred_element_type=F32)
        et = jnp.exp(lgt - jnp.max(lgt, axis=0, keepdims=True))
        afft_ref[0, sub] = et / jnp.sum(et, axis=0, keepdims=True)


def _merge_call(layer, xs, ya, yb, gates, modt, wa, wb, wo, norm2_g, wrt, n_lat):
    b, t, d = _stream_shape(xs)
    nt = t // TOKEN_BLOCK
    n_ctx = t - n_lat
    nh = ya.shape[1]
    ne = wrt.shape[1]
    w_spec = pl.BlockSpec((1, d, d), lambda bi, ti: (layer, 0, 0))
    out_shape = [jax.ShapeDtypeStruct((b, t, d), F32), jax.ShapeDtypeStruct((b, t, d), BF16),
                 jax.ShapeDtypeStruct((b, nt, ne, TOKEN_BLOCK), F32)]

    def call(rows, first_block, n_steps, stream, prev):
        nsub = rows // TOKEN_BLOCK
        head_spec = pl.BlockSpec((1, nh, LANES, rows), lambda bi, ti: (bi, 0, 0, first_block + ti))
        tok_spec = pl.BlockSpec((1, rows, d), lambda bi, ti: (bi, first_block + ti, 0))
        x_arr, x_first = _stream_part(xs, stream, first_block)
        operands = (x_arr, ya, yb, gates, modt, wa, wb, wo, norm2_g, wrt) + tuple(prev)
        return pl.pallas_call(
            functools.partial(_merge_kernel, d=d),
            grid=(b, n_steps),
            in_specs=[pl.BlockSpec((1, rows, d), lambda bi, ti: (bi, x_first + ti, 0)), head_spec, head_spec,
                      pl.BlockSpec((1, rows, 2 * d), lambda bi, ti: (bi, first_block + ti, 0)),
                      pl.BlockSpec((1, 1, 6, d), lambda bi, ti: (bi, stream, 0, 0)),
                      w_spec, w_spec, w_spec,
                      pl.BlockSpec((1, 1, d), lambda bi, ti: (layer, 0, 0)),
                      pl.BlockSpec((1, ne, d), lambda bi, ti: (layer, 0, 0))]
                     + [pl.BlockSpec(memory_space=pl.ANY)] * len(prev),
            out_specs=[tok_spec, tok_spec,
                       pl.BlockSpec((1, nsub, ne, TOKEN_BLOCK), lambda bi, ti: (bi, first_block + ti, 0, 0))],
            out_shape=out_shape,
            input_output_aliases={10 + n: n for n in range(len(prev))},
            compiler_params=_cparams(2),
            name="merge",
        )(*operands)

    rows = next(u for u in (2 * TOKEN_BLOCK, TOKEN_BLOCK) if n_lat % u == 0)
    outs = call(rows, 0, n_lat // rows, 0, ())
    return call(n_ctx, n_lat // n_ctx, 1, 1, outs)


def _route_kernel(afft_ref, tri_ref, pos_ref, cnt_ref, *, nlb, cap_l, cap_c):
    nt = afft_ref.shape[1]
    ne = afft_ref.shape[2]
    tri = tri_ref[...]

    def bits(blk):
        return lax.bitcast_convert_type(afft_ref[0, blk], I32)

    def count(pred_fn, blocks):
        acc = jnp.zeros((ne, TOKEN_BLOCK), F32)
        for blk in blocks:
            acc = acc + jnp.where(pred_fn(bits(blk)), 1.0, 0.0)
        return jnp.sum(acc, axis=1, keepdims=True)

    def select(blocks, cap, base, with_counts):
        def it(k, thr):
            cand = thr | jnp.left_shift(jnp.int32(1), 30 - k)
            return jnp.where(count(lambda bt: bt >= cand, blocks) >= cap, cand, thr)
        thr = lax.fori_loop(0, 31, it, jnp.zeros((ne, 1), I32))
        ties_wanted = cap - count(lambda bt: bt > thr, blocks)
        seen_eq = jnp.zeros((ne, 1), F32)
        seen_sel = jnp.zeros((ne, 1), F32)
        lane = lax.broadcasted_iota(I32, (ne, LANES), 1)
        cntv = jnp.zeros((ne, LANES), F32)
        for n, blk in enumerate(blocks):
            bt = bits(blk)
            eq = bt == thr
            eqf = jnp.where(eq, 1.0, 0.0)
            rank = jnp.dot(eqf.astype(BF16), tri, preferred_element_type=F32) + seen_eq
            self_ = jnp.where(eq, jnp.where(rank < ties_wanted, 1.0, 0.0), jnp.where(bt > thr, 1.0, 0.0))
            sel = self_ > 0.5
            slot = jnp.dot(self_.astype(BF16), tri, preferred_element_type=F32) + seen_sel
            pos_ref[0, blk] = jnp.where(sel, slot.astype(I32) + base, -1)
            if with_counts:
                cntv = jnp.where(lane == n, seen_sel, cntv)
            seen_eq = seen_eq + jnp.sum(eqf, axis=1, keepdims=True)
            seen_sel = seen_sel + jnp.sum(self_, axis=1, keepdims=True)
        if with_counts:
            cntv = jnp.where(lane == len(blocks), seen_sel, cntv)
            cnt_ref[0] = cntv.astype(I32)

    select(list(range(nlb)), cap_l, 0, True)
    select(list(range(nlb, nt)), cap_c, cap_l, False)


def _route_call(afft, tri, n_lat, cap_l, cap_c):
    b, nt, ne, _ = afft.shape
    return pl.pallas_call(
        functools.partial(_route_kernel, nlb=n_lat // TOKEN_BLOCK, cap_l=cap_l, cap_c=cap_c),
        grid=(b,),
        in_specs=[pl.BlockSpec((1, nt, ne, TOKEN_BLOCK), lambda bi: (bi, 0, 0, 0)),
                  pl.BlockSpec((TOKEN_BLOCK, TOKEN_BLOCK), lambda bi: (0, 0))],
        out_specs=[pl.BlockSpec((1, nt, ne, TOKEN_BLOCK), lambda bi: (bi, 0, 0, 0)),
                   pl.BlockSpec((1, ne, LANES), lambda bi: (bi, 0, 0))],
        out_shape=[jax.ShapeDtypeStruct(afft.shape, I32), jax.ShapeDtypeStruct((b, ne, LANES), I32)],
        compiler_params=_cparams(1),
        name="route",
    )(afft, tri)


def _slot_onehot(pos_row, first_slot, n_slots):
    slots = first_slot + lax.broadcasted_iota(I32, (n_slots, pos_row.shape[1]), 0)
    return jnp.where(pos_row == slots, 1.0, 0.0).astype(BF16)


def _gather_kernel(cnt_ref, h2_ref, pos_ref, xe_ref, *, nlb, cap_l, cap_c):
    for k in range(xe_ref.shape[1]):
        _gather_expert(k, cnt_ref, h2_ref, pos_ref, xe_ref, nlb=nlb, cap_l=cap_l, cap_c=cap_c)


def _gather_expert(k, cnt_ref, h2_ref, pos_ref, xe_ref, *, nlb, cap_l, cap_c):
    bi = pl.program_id(0)
    e = pl.program_id(1) * xe_ref.shape[1] + k
    ne = pl.num_programs(1) * xe_ref.shape[1]
    cbase = (bi * ne + e) * (nlb + 1)
    win = GATHER_WINDOW
    xe_ref[0, k, :cap_l, :] = jnp.zeros((cap_l, xe_ref.shape[3]), BF16)

    def slot_range(blk):
        return cnt_ref[cbase + blk], cnt_ref[cbase + blk + 1]

    def window_start(lo):
        return jnp.minimum((lo // BF16_ROWS) * BF16_ROWS, cap_l - win)

    def add_rows(blk, t0, s0, n):
        onehot = _slot_onehot(pos_ref[0, blk, pl.ds(e, 1), :], s0, n)
        xe_ref[0, k, pl.ds(s0, n), :] += jnp.dot(onehot, h2_ref[0, pl.ds(t0, TOKEN_BLOCK), :],
                                                 preferred_element_type=F32).astype(BF16)

    fits = jnp.bool_(True)
    for blk in range(nlb):
        lo, hi = slot_range(blk)
        fits = fits & (hi <= window_start(lo) + win)

    @pl.when(fits)
    def _():
        for blk in range(nlb):
            s0 = pl.multiple_of(window_start(slot_range(blk)[0]), BF16_ROWS)
            add_rows(blk, blk * TOKEN_BLOCK, s0, win)

    @pl.when(jnp.logical_not(fits))
    def _():
        def body(blk, _):
            lo, hi = slot_range(blk)

            def chunk(c, _):
                add_rows(blk, pl.multiple_of(blk * TOKEN_BLOCK, TOKEN_BLOCK),
                         pl.multiple_of(c * SLOT_CHUNK, SLOT_CHUNK), SLOT_CHUNK)
                return 0

            lax.fori_loop(lo // SLOT_CHUNK, jnp.where(hi > lo, (hi + SLOT_CHUNK - 1) // SLOT_CHUNK, 0), chunk, 0)
            return 0

        lax.fori_loop(0, nlb, body, 0)

    if cap_c:
        ctx_acc = jnp.zeros((cap_c, h2_ref.shape[2]), F32)
        for blk in range(nlb, pos_ref.shape[1]):
            onehot = _slot_onehot(pos_ref[0, blk, pl.ds(e, 1), :], cap_l, cap_c)
            t0 = blk * TOKEN_BLOCK
            ctx_acc = ctx_acc + jnp.dot(onehot, h2_ref[0, t0:t0 + TOKEN_BLOCK, :], preferred_element_type=F32)
        xe_ref[0, k, cap_l:cap_l + cap_c, :] = ctx_acc.astype(BF16)


def _gather_call(cnt_flat, h2, pos, n_lat, cap_l, cap_c):
    b, t, d = h2.shape
    _, nt, ne, _ = pos.shape
    slots = cap_l + cap_c
    eps = 2 if ne % 2 == 0 else 1
    return pl.pallas_call(
        functools.partial(_gather_kernel, nlb=n_lat // TOKEN_BLOCK, cap_l=cap_l, cap_c=cap_c),
        grid_spec=pltpu.PrefetchScalarGridSpec(
            num_scalar_prefetch=1, grid=(b, ne // eps),
            in_specs=[pl.BlockSpec((1, t, d), lambda bi, ei, s: (bi, 0, 0)),
                      pl.BlockSpec((1, nt, ne, TOKEN_BLOCK), lambda bi, ei, s: (bi, 0, 0, 0))],
            out_specs=pl.BlockSpec((1, eps, slots, d), lambda bi, ei, s: (bi, ei, 0, 0))),
        out_shape=jax.ShapeDtypeStruct((b, ne, slots, d), BF16),
        compiler_params=_cparams(2),
        name="moe_gather",
    )(cnt_flat, h2, pos)


def _ffn_kernel(x_ref, w1_ref, w3_ref, w2_ref, o_ref):
    ff = w1_ref.shape[-1]
    for s in range(x_ref.shape[0]):
        x = x_ref[s, 0]
        acc = None
        for f0 in range(0, ff, FF_CHUNK):
            f1 = min(ff, f0 + FF_CHUNK)
            a = jnp.dot(x, w1_ref[0, 0, :, f0:f1], preferred_element_type=F32)
            g = jnp.dot(x, w3_ref[0, 0, :, f0:f1], preferred_element_type=F32)
            hid = (a * jax.nn.sigmoid(a) * g).astype(BF16)
            part = jnp.dot(hid, w2_ref[0, 0, f0:f1, :], preferred_element_type=F32)
            acc = part if acc is None else acc + part
        o_ref[s, 0] = acc.astype(BF16)


def _ffn_call(layer, xe, w1, w3, w2):
    b, ne, slots, d = xe.shape
    ff = w1.shape[-1]
    spb = 2 if b % 2 == 0 else 1
    x_spec = pl.BlockSpec((spb, 1, slots, d), lambda ei, bi: (bi, ei, 0, 0))
    return pl.pallas_call(
        _ffn_kernel,
        grid=(ne, b // spb),
        in_specs=[x_spec,
                  pl.BlockSpec((1, 1, d, ff), lambda ei, bi: (layer, ei, 0, 0)),
                  pl.BlockSpec((1, 1, d, ff), lambda ei, bi: (layer, ei, 0, 0)),
                  pl.BlockSpec((1, 1, ff, d), lambda ei, bi: (layer, ei, 0, 0))],
        out_specs=x_spec,
        out_shape=jax.ShapeDtypeStruct(xe.shape, BF16),
        compiler_params=_cparams(2),
        name="moe_ffn",
    )(xe, w1, w3, w2)


def _combine_kernel(cnt_ref, x_ref, ye_ref, pos_ref, afft_ref, mod_ref, *rest, first_tile, nlb, cap_l, cap_c):
    xo_ref, acc_ref = rest[-2:]
    for sub in range(x_ref.shape[1] // TOKEN_BLOCK):
        _combine_tile(sub, first_tile, cnt_ref, x_ref, ye_ref, pos_ref, afft_ref, mod_ref, xo_ref, acc_ref,
                      nlb=nlb, cap_l=cap_l, cap_c=cap_c)


def _combine_tile(sub, first_tile, cnt_ref, x_ref, ye_ref, pos_ref, afft_ref, mod_ref, xo_ref, acc_ref, *,
                  nlb, cap_l, cap_c):
    bi = pl.program_id(0)
    ti = first_tile + pl.program_id(1) * (x_ref.shape[1] // TOKEN_BLOCK) + sub
    rows = slice(sub * TOKEN_BLOCK, (sub + 1) * TOKEN_BLOCK)
    ne = ye_ref.shape[1]
    is_lat = ti < nlb
    tn = (((0,), (0,)), ((), ()))
    win = GATHER_WINDOW

    def slot_range(e):
        cbase = (bi * ne + e) * (nlb + 1) + jnp.minimum(ti, nlb - 1)
        return cnt_ref[cbase], cnt_ref[cbase + 1]

    def window_start(lo):
        return jnp.minimum((lo // BF16_ROWS) * BF16_ROWS, cap_l - win)

    def gated_onehot(e, first_slot, n_slots):
        slots = first_slot + lax.broadcasted_iota(I32, (n_slots, TOKEN_BLOCK), 0)
        return jnp.where(pos_ref[0, sub, e:e + 1, :] == slots, afft_ref[0, sub, e:e + 1, :], 0.0).astype(BF16)

    def residual(y):
        xo_ref[0, rows] = x_ref[0, rows] + mod_ref[0, 0][5:6] * y

    def scatter(pieces):
        w = jnp.concatenate([p[0] for p in pieces], axis=0)
        y = jnp.concatenate([p[1] for p in pieces], axis=0)
        residual(lax.dot_general(w, y, tn, preferred_element_type=F32))

    if first_tile >= nlb:
        scatter([(gated_onehot(e, cap_l, cap_c), ye_ref[0, e, :cap_c, :]) for e in range(ne)])
        return

    fits = is_lat
    for e in range(ne):
        lo, hi = slot_range(e)
        fits = fits & (hi <= window_start(lo) + win)

    @pl.when(fits)
    def _():
        pieces = []
        for e in range(ne):
            s0 = pl.multiple_of(window_start(slot_range(e)[0]), BF16_ROWS)
            pieces.append((gated_onehot(e, s0, win), ye_ref[0, e, pl.ds(s0, win), :]))
        scatter(pieces)

    @pl.when(is_lat & jnp.logical_not(fits))
    def _():
        acc_ref[...] = jnp.zeros_like(acc_ref)
        for e in range(ne):
            lo, hi = slot_range(e)

            def body(c, _, e=e):
                s0 = pl.multiple_of(c * SLOT_CHUNK, SLOT_CHUNK)
                acc_ref[...] += lax.dot_general(gated_onehot(e, s0, SLOT_CHUNK),
                                                ye_ref[0, e, pl.ds(s0, SLOT_CHUNK), :], tn,
                                                preferred_element_type=F32)
                return 0

            lax.fori_loop(lo // SLOT_CHUNK, jnp.where(hi > lo, (hi + SLOT_CHUNK - 1) // SLOT_CHUNK, 0), body, 0)
        residual(acc_ref[...])


def _combine_call(cnt_flat, xs, ye, pos, afft, modt, n_lat, cap_l, cap_c, latent_only):
    b, t, d = xs.shape
    _, nt, ne, _ = pos.shape
    nlb = n_lat // TOKEN_BLOCK
    n_ctx = t - n_lat
    out_rows = n_lat if latent_only else t
    assert cap_c == 0 or cap_l % cap_c == 0

    def call(rows, first_block, n_steps, stream, prev):
        nsub = rows // TOKEN_BLOCK
        tok_spec = pl.BlockSpec((1, rows, d), lambda bi, ti, s: (bi, first_block + ti, 0))
        blk_spec = pl.BlockSpec((1, nsub, ne, TOKEN_BLOCK), lambda bi, ti, s: (bi, first_block + ti, 0, 0))
        if stream == 0:
            ye_spec = pl.BlockSpec((1, ne, cap_l, d), lambda bi, ti, s: (bi, 0, 0, 0))
        else:
            ye_spec = pl.BlockSpec((1, ne, cap_c, d), lambda bi, ti, s: (bi, 0, cap_l // cap_c, 0))
        operands = (cnt_flat, xs, ye, pos, afft, modt) + tuple(prev)
        return pl.pallas_call(
            functools.partial(_combine_kernel, first_tile=first_block * nsub, nlb=nlb, cap_l=cap_l, cap_c=cap_c),
            grid_spec=pltpu.PrefetchScalarGridSpec(
                num_scalar_prefetch=1, grid=(b, n_steps),
                in_specs=[tok_spec, ye_spec, blk_spec, blk_spec,
                          pl.BlockSpec((1, 1, 6, d), lambda bi, ti, s: (bi, stream, 0, 0))]
                         + [pl.BlockSpec(memory_space=pl.ANY)] * len(prev),
                out_specs=tok_spec,
                scratch_shapes=[pltpu.VMEM((TOKEN_BLOCK, d), F32)]),
            out_shape=jax.ShapeDtypeStruct((b, out_rows, d), F32),
            input_output_aliases={6 + n: n for n in range(len(prev))},
            compiler_params=_cparams(2),
            name="moe_combine",
        )(*operands)

    rows = next(u for u in (2 * TOKEN_BLOCK, TOKEN_BLOCK) if n_lat % u == 0)
    out = call(rows, 0, n_lat // rows, 0, ())
    return out if latent_only else call(n_ctx, n_lat // n_ctx, 1, 1, (out,))


def _rope_tables(n_lat, n_ctx):
    rows = n_lat // GRID_W
    row = jnp.repeat(jnp.arange(rows), GRID_W).astype(F32)
    col = jnp.tile(jnp.arange(GRID_W), rows).astype(F32)
    half = HEAD_DIM // 2
    inv_freq = ROPE_THETA ** (-jnp.arange(0, half, 2, dtype=F32) / half)
    ar = row[:, None] * inv_freq
    ac = col[:, None] * inv_freq
    cos64 = jnp.concatenate([jnp.cos(ar), jnp.cos(ar), jnp.cos(ac), jnp.cos(ac)], axis=1)
    sin64 = jnp.concatenate([-jnp.sin(ar), jnp.sin(ar), -jnp.sin(ac), jnp.sin(ac)], axis=1)
    cos_t = jnp.concatenate([jnp.tile(cos64, (1, LANES // HEAD_DIM)), jnp.ones((n_ctx, LANES), F32)], axis=0)
    sin_t = jnp.concatenate([jnp.tile(sin64, (1, LANES // HEAD_DIM)), jnp.zeros((n_ctx, LANES), F32)], axis=0)
    return cos_t, sin_t


def _swa_head_order():
    heads = []
    for j in range(SWA_KV_HEADS // 2):
        for g in range(SWA_GROUP):
            heads += [2 * SWA_GROUP * j + g, 2 * SWA_GROUP * j + SWA_GROUP + g]
    return np.concatenate([np.arange(h * HEAD_DIM, (h + 1) * HEAD_DIM) for h in heads])


def kernel(x, c, ctx, c_ctx, w_ada, b_ada, norm1_g, w_in, b_gate, diff_q_g, diff_k_g, diff_lambda, diff_subln_g,
           swa_q_g, swa_k_g, swa_sink, w_branch_a, w_branch_b, w_out, norm2_g, w_router, w_e1, w_e3, w_e2):
    b, n_lat, d = x.shape
    n_ctx = ctx.shape[1]
    depth = w_ada.shape[0]
    assert n_lat % TOKEN_BLOCK == 0 and n_ctx % TOKEN_BLOCK == 0 and n_lat >= 2 * TOKEN_BLOCK
    assert d == DIFF_HEADS * 2 * HEAD_DIM == SWA_Q_HEADS * HEAD_DIM and w_router.shape[-1] == N_EXPERTS
    cap_l = CAPACITY_FACTOR * n_lat // N_EXPERTS
    cap_c = CAPACITY_FACTOR * n_ctx // N_EXPERTS
    assert cap_l % SLOT_CHUNK == 0 and cap_c % 16 == 0
    nlb = n_lat // TOKEN_BLOCK

    perm = _swa_head_order()
    kvw = SWA_KV_HEADS * HEAD_DIM
    o = np.cumsum([0, d, d, d, d, kvw, kvw, d, d])
    w_in_p = jnp.concatenate(
        [w_in[..., o[0]:o[3]], w_in[..., o[3]:o[4]][..., perm], w_in[..., o[6]:o[8]], w_in[..., o[4]:o[6]]],
        axis=-1).astype(BF16)
    wa = w_branch_a.astype(BF16)
    wb = w_branch_b[:, perm, :].astype(BF16)
    wo = w_out.astype(BF16)
    wrt =jnp.swapaxes(w_router, 1, 2).astype(BF16)
    w1 = w_e1.astype(BF16)
    w3 = w_e3.astype(BF16)
    w2 = w_e2.astype(BF16)
    scale = HEAD_DIM ** -0.5
    hg = jnp.stack([jnp.tile(diff_q_g, (1, d // HEAD_DIM)) * (scale * math.log2(math.e)),
                    jnp.tile(diff_k_g, (1, d // HEAD_DIM)),
                    jnp.tile(swa_q_g, (1, d // HEAD_DIM)) * (scale * math.log2(math.e)),
                    jnp.tile(swa_k_g, (1, d // HEAD_DIM))], axis=1)
    def score_bound(q_gain, k_gain):
        return (HEAD_DIM * scale * math.log2(math.e)) * jnp.max(jnp.abs(q_gain), axis=1) * jnp.max(jnp.abs(k_gain), axis=1)

    diff_bounded = (score_bound(diff_q_g, diff_k_g) <= SCORE_BOUND_LOG2).astype(I32)
    swa_bounded = ((score_bound(swa_q_g, swa_k_g) <= SCORE_BOUND_LOG2)
                   & (jnp.max(swa_sink, axis=1) * math.log2(math.e) <= SCORE_BOUND_LOG2)).astype(I32)
    subln = jnp.broadcast_to(diff_subln_g[:, :, None], (depth, LANES, LANES))
    g1n = norm1_g.reshape(depth, 1, d)
    g2n = norm2_g.reshape(depth, 1, d)
    bg = b_gate.reshape(depth, 1, 2 * d)
    gm = jnp.asarray(np.kron(np.eye(TOKEN_BLOCK // HEAD_DIM), np.ones((HEAD_DIM, HEAD_DIM))), BF16)
    tri = jnp.asarray(np.triu(np.ones((TOKEN_BLOCK, TOKEN_BLOCK)), 1), BF16)
    cos_t, sin_t = _rope_tables(n_lat, n_ctx)
    key_i = np.arange(2 * TOKEN_BLOCK)[:, None]
    qry_i = np.arange(TOKEN_BLOCK)[None, :]
    band = np.where(np.abs(qry_i + WINDOW - key_i) <= WINDOW, 0.0, NEG_INF)
    band = jnp.asarray(np.concatenate([band, np.zeros((n_ctx, TOKEN_BLOCK))], axis=0), BF16)

    rows = -(-(b + 1) // 8) * 8
    cc = jnp.concatenate([c, c_ctx[None], jnp.zeros((rows - b - 1, d), F32)], axis=0)
    mod_all = _ada_call(cc, w_ada, b_ada)

    xs = (x, ctx)
    for i in range(depth):
        lam_init = 0.8 - 0.6 * math.exp(-0.3 * i)
        mod_l = mod_all[i, :b].reshape(b, 1, 6, d)
        mod_c = jnp.broadcast_to(mod_all[i, b].reshape(1, 1, 6, d), (b, 1, 6, d))
        modt = jnp.concatenate([mod_l, mod_c], axis=1)
        qa, ka, va, qb, gates, kb, vb = _inproj_call(i, xs, modt, g1n, w_in_p, gm, hg, bg, cos_t, sin_t, n_lat)
        ya = _diff_attn_call(i, diff_bounded[i:i + 1], qa, ka, va, diff_lambda, subln, n_lat, lam_init)
        yb = _swa_call(i, swa_bounded[i:i + 1], swa_sink[i], qb, kb, vb, band, n_lat)
        xs, h2, afft = _merge_call(i, xs, ya, yb, gates, modt, wa, wb, wo, g2n, wrt, n_lat)
        pos, cnt = _route_call(afft, tri, n_lat, cap_l, cap_c)
        cnt_flat = cnt[:, :, :nlb + 1].reshape(-1)
        last = i == depth - 1
        cap_ctx = 0 if last else cap_c
        xe = _gather_call(cnt_flat, h2, pos, n_lat, cap_l, cap_ctx)
        ye = _ffn_call(i, xe, w1, w3, w2)
        xs = _combine_call(cnt_flat, xs, ye, pos, afft, modt, n_lat, cap_l, cap_ctx, latent_only=last)
    return xs
```
